```python
import math
import jax, jax.numpy as jnp
from jax import lax
import numpy as np

D_MODEL = 2048
BATCH = 4
SEQ = 2048
DEPTH = 4
DEC_BATCH = 8
DEC_SEQ = 8
PAST_LEN = 16384
PAGE_SIZE = 128

MIX_W = D_MODEL
ATT_W = MIX_W // 2
HEAD_DIM = 64
N_HEADS = ATT_W // HEAD_DIM
CONV_CH = MIX_W - ATT_W
CONV_K = 31
DILATED_BRANCHES = ((128, 1), (512, 4), (2048, 16))
MAX_WINDOW = max(w for w, _ in DILATED_BRANCHES)
D_FF = ((8 * D_MODEL + 3 * 256 - 1) // (3 * 256)) * 256
N_MOD = 6
EPS = 1e-6
ATTN_SCALE = HEAD_DIM ** -0.5
NEG = -1e30

kernel_name = 'hymba_dilated_conformer_decoder_step'


def alibi_slopes(n):
    def pow2(m):
        start = 2.0 ** (-8.0 / m)
        return [start ** (i + 1) for i in range(m)]
    if math.log2(n).is_integer():
        s = pow2(n)
    else:
        m = 2 ** math.floor(math.log2(n))
        s = pow2(m) + pow2(2 * m)[0::2][: n - m]
    return jnp.asarray(np.array(s, dtype=np.float32))


def rms_norm(x, g):
    x32 = x.astype(jnp.float32)
    y = x32 * lax.rsqrt(jnp.mean(x32 * x32, axis=-1, keepdims=True) + EPS)
    return (y * g.astype(jnp.float32)).astype(x.dtype)


def layer_norm(x, g, b):
    x32 = x.astype(jnp.float32)
    mu = jnp.mean(x32, axis=-1, keepdims=True)
    xc = x32 - mu
    y = xc * lax.rsqrt(jnp.mean(xc * xc, axis=-1, keepdims=True) + EPS)
    return (y * g.astype(jnp.float32) + b.astype(jnp.float32)).astype(x.dtype)


def dilated_branch_prompt(q, k, v, window, dilation, slopes):
    B, S, H, Dh = q.shape
    steps = window // dilation
    L = steps
    span = dilation * L
    Sp = -(-S // span) * span
    A = Sp // dilation
    nb = A // L

    def strided(x):
        x = jnp.pad(x, ((0, 0), (0, Sp - S), (0, 0), (0, 0)))
        x = x.reshape(B, A, dilation, H, Dh).transpose(0, 2, 1, 3, 4)
        return x.reshape(B * dilation, nb, L, H, Dh)

    def with_prev(x):
        prev = jnp.pad(x[:, :-1], ((0, 0), (1, 0), (0, 0), (0, 0), (0, 0)))
        return jnp.concatenate([prev, x], axis=2)

    qs = strided(q)
    kc = with_prev(strided(k))
    vc = with_prev(strided(v))
    s = jnp.einsum('bnqhd,bnkhd->bnhqk', qs, kc) * ATTN_SCALE
    qi = jnp.arange(L)[:, None]
    ki = jnp.arange(2 * L)[None, :]
    j = L + qi - ki
    blk = jnp.arange(nb)[:, None, None]
    valid = (j >= 0)[None] & (j <= steps)[None] & ((blk * L - L + ki[None]) >= 0)
    bias = -slopes[:, None, None] * (j * dilation).astype(jnp.float32)[None]
    s = jnp.where(valid[None, :, None, :, :], s + bias[None, None], NEG)
    lse = jax.nn.logsumexp(s, axis=-1)
    p = jnp.exp(s - lse[..., None])
    o = jnp.einsum('bnhqk,bnkhd->bnqhd', p, vc)

    def unstride(x):
        rest = x.shape[3:]
        x = x.reshape((B, dilation, A) + rest)
        x = jnp.moveaxis(x, 1, 2)
        return x.reshape((B, Sp) + rest)[:, :S]

    o = unstride(o)
    lse = unstride(lse.transpose(0, 1, 3, 2)[..., None])[..., 0]
    return o, lse


def dilated_branch_decode(q, k_ext, v_ext, n_past, window, dilation, slopes):
    T = q.shape[1]
    steps = window // dilation
    jj = jnp.arange(steps + 1)
    idx = n_past + jnp.arange(T)[:, None] - jj[None, :] * dilation
    valid = idx >= 0
    idx_c = jnp.maximum(idx, 0)
    kg = k_ext[:, idx_c]
    vg = v_ext[:, idx_c]
    s = jnp.einsum('bthd,btjhd->bhtj', q, kg) * ATTN_SCALE
    bias = -slopes[:, None, None] * (jj * dilation).astype(jnp.float32)[None, None, :]
    s = jnp.where(valid[None, None], s + bias[None], NEG)
    lse = jax.nn.logsumexp(s, axis=-1)
    p = jnp.exp(s - lse[..., None])
    o = jnp.einsum('bhtj,btjhd->bthd', p, vg)
    return o, lse.transpose(0, 2, 1)


def merge_branches(outs, lses):
    w = jax.nn.softmax(jnp.stack(lses), axis=0)
    return jnp.sum(w[..., None] * jnp.stack(outs), axis=0)


def attention_prompt(q, k, v, slopes):
    S = q.shape[1]
    q32, k32, v32 = q.astype(jnp.float32), k.astype(jnp.float32), v.astype(jnp.float32)
    outs, lses = [], []
    for window, dilation in DILATED_BRANCHES:
        o_i, l_i = dilated_branch_prompt(q32, k32, v32, window, dilation, slopes)
        outs.append(o_i)
        lses.append(l_i)
    keep = min(MAX_WINDOW, S)
    return merge_branches(outs, lses), k[:, S - keep:], v[:, S - keep:]


def attention_decode(q, k, v, past_k, past_v, slopes):
    n_past = past_k.shape[1]
    k_ext = jnp.concatenate([past_k, k.astype(past_k.dtype)], axis=1)
    v_ext = jnp.concatenate([past_v, v.astype(past_v.dtype)], axis=1)
    q32, k32, v32 = q.astype(jnp.float32), k_ext.astype(jnp.float32), v_ext.astype(jnp.float32)
    outs, lses = [], []
    for window, dilation in DILATED_BRANCHES:
        o_i, l_i = dilated_branch_decode(q32, k32, v32, n_past, window, dilation, slopes)
        outs.append(o_i)
        lses.append(l_i)
    return merge_branches(outs, lses), k_ext[:, -n_past:], v_ext[:, -n_past:]


def conformer_conv(a, g, past, conv_w, conv_b, ln_g, ln_b):
    u = a * jax.nn.sigmoid(g)
    if past is None:
        past = jnp.zeros((u.shape[0], CONV_K - 1, CONV_CH), u.dtype)
    ext = jnp.concatenate([past.astype(u.dtype), u], axis=1)
    y = lax.conv_general_dilated(ext, conv_w[:, None, :].astype(u.dtype), (1,), 'VALID',
                                 dimension_numbers=('NWC', 'WIO', 'NWC'),
                                 feature_group_count=CONV_CH) + conv_b
    y = jax.nn.silu(layer_norm(y, ln_g, ln_b))
    return y, ext[:, -(CONV_K - 1):]


def layer(x, c, past_k, past_v, past_conv, slopes, g_pre_mix, g_post_mix, g_pre_ffn, g_post_ffn,
          w_mod, b_mod, w_in, g_attn_out, conv_w, conv_b, conv_ln_g, conv_ln_b, w_out, w_gu, w_down):
    B, T, _ = x.shape
    mod = jax.nn.silu(c) @ w_mod + b_mod
    sh_a, sc_a, gt_a, sh_f, sc_f, gt_f = [m[:, None, :] for m in jnp.split(mod, N_MOD, axis=-1)]

    h = rms_norm(x, g_pre_mix) * (1 + sc_a) + sh_a
    proj = h @ w_in
    q, k, v, a, g = jnp.split(proj, [ATT_W, 2 * ATT_W, 3 * ATT_W, 3 * ATT_W + CONV_CH], axis=-1)
    q = q.reshape(B, T, N_HEADS, HEAD_DIM)
    k = k.reshape(B, T, N_HEADS, HEAD_DIM)
    v = v.reshape(B, T, N_HEADS, HEAD_DIM)
    if past_k is None:
        o, new_k, new_v = attention_prompt(q, k, v, slopes)
    else:
        o, new_k, new_v = attention_decode(q, k, v, past_k, past_v, slopes)
    o = rms_norm(o.reshape(B, T, ATT_W).astype(x.dtype), g_attn_out)
    y, new_conv = conformer_conv(a, g, past_conv, conv_w, conv_b, conv_ln_g, conv_ln_b)
    mix = jnp.concatenate([o, y.astype(o.dtype)], axis=-1) @ w_out
    x = x + gt_a * rms_norm(mix, g_post_mix)

    h2 = rms_norm(x, g_pre_ffn) * (1 + sc_f) + sh_f
    gate, up = jnp.split(h2 @ w_gu, 2, axis=-1)
    f = (jax.nn.silu(gate) * up) @ w_down
    x = x + gt_f * rms_norm(f, g_post_ffn)
    return x, new_k, new_v, new_conv


def setup_inputs(seed: int = 0) -> dict:
    key = jax.random.key(seed)
    ks = jax.random.split(key, 24)
    wbuf = min(MAX_WINDOW, PAST_LEN)

    def nrm(k, shape, s):
        return s * jax.random.normal(k, shape, jnp.float32)

    return {
        'x_prompt': nrm(ks[0], (BATCH, SEQ, D_MODEL), 1.0),
        'x_sample': nrm(ks[1], (DEC_BATCH, DEC_SEQ, D_MODEL), 1.0),
        'c_prompt': nrm(ks[2], (BATCH, D_MODEL), 1.0),
        'c_sample': nrm(ks[3], (DEC_BATCH, D_MODEL), 1.0),
        'cache_win_k': nrm(ks[4], (DEPTH, DEC_BATCH, wbuf, N_HEADS, HEAD_DIM), 1.0),
        'cache_win_v': nrm(ks[5], (DEPTH, DEC_BATCH, wbuf, N_HEADS, HEAD_DIM), 1.0),
        'state_conv': nrm(ks[6], (DEPTH, DEC_BATCH, CONV_K - 1, CONV_CH), 0.5),
        'g_pre_mix': 1.0 + nrm(ks[7], (DEPTH, D_MODEL), 0.02),
        'g_post_mix': 1.0 + nrm(ks[8], (DEPTH, D_MODEL), 0.02),
        'g_pre_ffn': 1.0 + nrm(ks[9], (DEPTH, D_MODEL), 0.02),
        'g_post_ffn': 1.0 + nrm(ks[10], (DEPTH, D_MODEL), 0.02),
        'w_mod': nrm(ks[11], (DEPTH, D_MODEL, N_MOD * D_MODEL), D_MODEL ** -0.5),
        'b_mod': nrm(ks[12], (DEPTH, N_MOD * D_MODEL), 0.01),
        'w_in': nrm(ks[13], (DEPTH, D_MODEL, 3 * ATT_W + 2 * CONV_CH), D_MODEL ** -0.5),
        'g_attn_out': 1.0 + nrm(ks[14], (DEPTH, ATT_W), 0.02),
        'conv_w': nrm(ks[15], (DEPTH, CONV_K, CONV_CH), CONV_K ** -0.5),
        'conv_b': nrm(ks[16], (DEPTH, CONV_CH), 0.01),
        'conv_ln_g': 1.0 + nrm(ks[17], (DEPTH, CONV_CH), 0.02),
        'conv_ln_b': nrm(ks[18], (DEPTH, CONV_CH), 0.01),
        'w_out': nrm(ks[19], (DEPTH, MIX_W, D_MODEL), MIX_W ** -0.5),
        'w_gu': nrm(ks[20], (DEPTH, D_MODEL, 2 * D_FF), D_MODEL ** -0.5),
        'w_down': nrm(ks[21], (DEPTH, D_FF, D_MODEL), D_FF ** -0.5),
    }


def reference(x_prompt, x_sample, c_prompt, c_sample, cache_win_k, cache_win_v, state_conv,
              g_pre_mix, g_post_mix, g_pre_ffn, g_post_ffn, w_mod, b_mod, w_in, g_attn_out,
              conv_w, conv_b, conv_ln_g, conv_ln_b, w_out, w_gu, w_down):
    slopes = alibi_slopes(N_HEADS)
    yp, ys = x_prompt, x_sample
    kp_l, vp_l, cp_l, ks_l, vs_l, cs_l = [], [], [], [], [], []
    for l in range(DEPTH):
        lw = (g_pre_mix[l], g_post_mix[l], g_pre_ffn[l], g_post_ffn[l], w_mod[l], b_mod[l],
              w_in[l], g_attn_out[l], conv_w[l], conv_b[l], conv_ln_g[l], conv_ln_b[l],
              w_out[l], w_gu[l], w_down[l])
        yp, kp, vp, cp = layer(yp, c_prompt, None, None, None, slopes, *lw)
        ys, kn, vn, cn = layer(ys, c_sample, cache_win_k[l], cache_win_v[l], state_conv[l], slopes, *lw)
        kp_l.append(kp)
        vp_l.append(vp)
        cp_l.append(cp)
        ks_l.append(kn)
        vs_l.append(vn)
        cs_l.append(cn)
    return (yp, ys, jnp.stack(kp_l), jnp.stack(vp_l), jnp.stack(cp_l),
            jnp.stack(ks_l), jnp.stack(vs_l), jnp.stack(cs_l))
```

```python
import functools
import math

import numpy as np
import jax
import jax.numpy as jnp
from jax import lax
from jax.experimental import pallas as pl
from jax.experimental.pallas import tpu as pltpu

D_MODEL = 2048
ATT_W = 1024
HEAD_DIM = 64
N_HEADS = 16
CONV_CH = 1024
CONV_K = 31
D_FF = 5632
N_MOD = 6
EPS = 1e-6
ATTN_SCALE = HEAD_DIM ** -0.5
NEG = -1e30
BRANCHES = ((128, 1), (512, 4), (2048, 16))
STEPS = 128
LANES = 128
HALO = 32
MOD_ROWS = 16
VMEM_LIMIT = 56 * 1024 * 1024

F32 = jnp.float32
BF16 = jnp.bfloat16


def _alibi_slopes(n):
    def pow2(m):
        start = 2.0 ** (-8.0 / m)
        return [start ** (i + 1) for i in range(m)]
    if math.log2(n).is_integer():
        s = pow2(n)
    else:
        m = 2 ** math.floor(math.log2(n))
        s = pow2(m) + pow2(2 * m)[0::2][: n - m]
    return np.array(s, dtype=np.float32)


def _rms(x, g):
    return x * lax.rsqrt(jnp.mean(x * x, axis=-1, keepdims=True) + EPS) * g


def _silu(x):
    return x * jax.nn.sigmoid(x)


def _params(sem):
    return pltpu.CompilerParams(dimension_semantics=sem, vmem_limit_bytes=VMEM_LIMIT)


def _resident(shape):
    nd = len(shape)
    return pl.BlockSpec(shape, lambda *_: (0,) * nd, pipeline_mode=pl.Buffered(1))


def _mod_kernel(c_ref, w_ref, b_ref, o_ref):
    s = _silu(c_ref[...]).astype(BF16)
    o_ref[...] = jnp.dot(s, w_ref[...].astype(BF16), preferred_element_type=F32) + b_ref[...]


def _modulation(c_all, w_mod, b_mod):
    depth = w_mod.shape[0]
    tn = 1024
    nj = D_MODEL // tn
    return pl.pallas_call(
        _mod_kernel,
        grid=(depth, N_MOD, nj),
        in_specs=[
            pl.BlockSpec((MOD_ROWS, D_MODEL), lambda l, m, j: (0, 0)),
            pl.BlockSpec((None, D_MODEL, tn), lambda l, m, j: (l, 0, m * nj + j)),
            pl.BlockSpec((None, 1, tn), lambda l, m, j: (l, 0, m * nj + j)),
        ],
        out_specs=pl.BlockSpec((None, None, MOD_ROWS, tn), lambda l, m, j: (l, m, 0, j)),
        out_shape=jax.ShapeDtypeStruct((depth, N_MOD, MOD_ROWS, D_MODEL), F32),
        compiler_params=_params(("arbitrary", "arbitrary", "arbitrary")),
        name="modulation",
    )(c_all, w_mod, b_mod.reshape(depth, 1, N_MOD * D_MODEL))


def _mod_rows(ref, per_row, tiles_per_seq):
    if per_row:
        return ref[...]
    return ref[pl.ds(pl.program_id(0) // tiles_per_seq, 1), :]


def _mod_spec(per_row, tm, l, m):
    if per_row:
        return pl.BlockSpec((None, None, tm, D_MODEL), lambda i, *_: (l, m, i, 0))
    return pl.BlockSpec((None, None, MOD_ROWS, D_MODEL), lambda i, *_: (l, m, 0, 0))


def _inproj_kernel(x_ref, sc_ref, sh_ref, g_ref, w_ref, q_ref, k_ref, v_ref, u_ref, *, per_row, tiles_per_seq):
    sc = _mod_rows(sc_ref, per_row, tiles_per_seq)
    sh = _mod_rows(sh_ref, per_row, tiles_per_seq)
    h = (_rms(x_ref[...], g_ref[...]) * (1.0 + sc) + sh).astype(BF16)

    def proj(c):
        return jnp.dot(h, w_ref[:, c * ATT_W:(c + 1) * ATT_W], preferred_element_type=F32)

    q_ref[...] = proj(0)
    k_ref[...] = proj(1)
    v_ref[...] = proj(2)
    u_ref[...] = proj(3) * jax.nn.sigmoid(proj(4))


def _inproj(x, mod, g_pre, w_in, l, *, tm, per_row, tiles_per_seq):
    m_rows = x.shape[0]
    row = lambda w: pl.BlockSpec((tm, w), lambda i: (i, 0))
    out = jax.ShapeDtypeStruct((m_rows, ATT_W), F32)
    return pl.pallas_call(
        functools.partial(_inproj_kernel, per_row=per_row, tiles_per_seq=tiles_per_seq),
        grid=(m_rows // tm,),
        in_specs=[
            row(D_MODEL),
            _mod_spec(per_row, tm, l, 1),
            _mod_spec(per_row, tm, l, 0),
            pl.BlockSpec((1, D_MODEL), lambda i: (0, 0)),
            _resident(w_in.shape),
        ],
        out_specs=[row(ATT_W)] * 4,
        out_shape=[out] * 4,
        compiler_params=_params(("arbitrary",)),
        name="inproj",
    )(x, mod, mod, g_pre, w_in)


def _attn_prompt_kernel(slopes_ref, q_ref, k_ref, v_ref, o_ref, ob_scr, lse_scr, bias_scr, *, seq):
    hp = pl.program_id(1)
    lane = lax.broadcasted_iota(jnp.int32, (1, LANES), 1)
    head_mask = (lane < HEAD_DIM, lane >= HEAD_DIM)

    qi = lax.broadcasted_iota(jnp.int32, (STEPS, 2 * STEPS), 0)
    ki = lax.broadcasted_iota(jnp.int32, (STEPS, 2 * STEPS), 1)
    j = STEPS + qi - ki
    valid = (j >= 0) & (j <= STEPS)
    jf = j.astype(F32)
    for h in range(2):
        slope = slopes_ref[2 * hp + h]
        for bi, (_, d) in enumerate(BRANCHES):
            bias_scr[h * 3 + bi] = jnp.where(valid, -(slope * d) * jf, NEG)

    def rows_of(start, d):
        return pl.ds(start, STEPS) if d == 1 else pl.ds(start, STEPS, stride=d)

    def block(bi, d, start, first):
        own = rows_of(start, d)
        q = q_ref[own, :] * ATTN_SCALE
        if first:
            kc = k_ref[own, :].astype(BF16)
            vc = v_ref[own, :].astype(BF16)
        else:
            prev = rows_of(start - STEPS * d, d)
            kc = jnp.concatenate([k_ref[prev, :], k_ref[own, :]], axis=0).astype(BF16)
            vc = jnp.concatenate([v_ref[prev, :], v_ref[own, :]], axis=0).astype(BF16)
        outs, lses = [], []
        for h in range(2):
            qh = jnp.where(head_mask[h], q, 0.0).astype(BF16)
            s = lax.dot_general(qh, kc, (((1,), (1,)), ((), ())), preferred_element_type=F32)
            if first:
                s = s + bias_scr[h * 3 + bi, :, STEPS:]
            else:
                s = s + bias_scr[h * 3 + bi]
            m = jnp.max(s, axis=-1, keepdims=True)
            p = jnp.exp(s - m)
            lsum = jnp.sum(p, axis=-1, keepdims=True)
            o = jnp.dot(p.astype(BF16), vc, preferred_element_type=F32)
            outs.append(o / lsum)
            lses.append(m + jnp.log(lsum))
        ob_scr[bi, own, :] = jnp.where(head_mask[0], outs[0], outs[1])
        lse_scr[bi, own, :] = jnp.where(head_mask[0], lses[0], lses[1])

    for bi, (_, d) in enumerate(BRANCHES):
        nb = seq // (STEPS * d)

        def residue(r, carry, bi=bi, d=d, nb=nb):
            block(bi, d, r, True)

            def later(n, c):
                block(bi, d, r + n * (STEPS * d), False)
                return c
            if nb > 1:
                lax.fori_loop(1, nb, later, 0)
            return carry
        if d == 1:
            residue(0, 0)
        else:
            lax.fori_loop(0, d, residue, 0)

    rc = 256
    def merge(c, carry):
        rows = pl.ds(pl.multiple_of(c * rc, rc), rc)
        l0, l1, l2 = lse_scr[0, rows, :], lse_scr[1, rows, :], lse_scr[2, rows, :]
        m = jnp.maximum(jnp.maximum(l0, l1), l2)
        e0, e1, e2 = jnp.exp(l0 - m), jnp.exp(l1 - m), jnp.exp(l2 - m)
        num = e0 * ob_scr[0, rows, :] + e1 * ob_scr[1, rows, :] + e2 * ob_scr[2, rows, :]
        o_ref[rows, :] = num / (e0 + e1 + e2)
        return carry
    lax.fori_loop(0, seq // rc, merge, 0)


def _attn_prompt(slopes, q, k, v):
    b, seq, _ = q.shape
    blk = pl.BlockSpec((None, seq, LANES), lambda i, j: (i, 0, j))
    return pl.pallas_call(
        functools.partial(_attn_prompt_kernel, seq=seq),
        grid=(b, ATT_W // LANES),
        in_specs=[pl.BlockSpec(memory_space=pltpu.SMEM), blk, blk, blk],
        out_specs=blk,
        out_shape=jax.ShapeDtypeStruct(q.shape, F32),
        scratch_shapes=[
            pltpu.VMEM((3, seq, LANES), F32),
            pltpu.VMEM((3, seq, LANES), F32),
            pltpu.VMEM((6, STEPS, 2 * STEPS), F32),
        ],
        compiler_params=_params(("arbitrary", "arbitrary")),
        name="attn_prompt",
    )(slopes, q, k, v)


def _branch_count(delta):
    cnt = jnp.zeros(delta.shape, F32)
    for w, d in BRANCHES:
        ok = (delta >= 0) & (delta <= w) & ((delta & (d - 1)) == 0)
        cnt = cnt + ok.astype(F32)
    return cnt


def _attn_decode_kernel(slopes_ref, q_ref, kn_ref, vn_ref, kc_ref, vc_ref, o_ref, ko_ref, vo_ref, *, n_past, t_new):
    hp = pl.program_id(1)
    lane = lax.broadcasted_iota(jnp.int32, (1, LANES), 1)
    head_mask = (lane < HEAD_DIM, lane >= HEAD_DIM)

    ti = lax.broadcasted_iota(jnp.int32, (t_new, n_past), 0)
    pi = lax.broadcasted_iota(jnp.int32, (t_new, n_past), 1)
    delta_c = n_past + ti - pi
    cnt_c = _branch_count(delta_c)
    tn = lax.broadcasted_iota(jnp.int32, (t_new, t_new), 0)
    pn = lax.broadcasted_iota(jnp.int32, (t_new, t_new), 1)
    delta_n = tn - pn
    cnt_n = _branch_count(delta_n)

    q = q_ref[...] * ATTN_SCALE
    kc = kc_ref[...].astype(BF16)
    vc = vc_ref[...].astype(BF16)
    kn = kn_ref[...].astype(BF16)
    vn = vn_ref[...].astype(BF16)
    outs = []
    nt = (((1,), (1,)), ((), ()))
    for h in range(2):
        slope = slopes_ref[2 * hp + h]
        qh = jnp.where(head_mask[h], q, 0.0).astype(BF16)
        s_c = lax.dot_general(qh, kc, nt, preferred_element_type=F32)
        s_n = lax.dot_general(qh, kn, nt, preferred_element_type=F32)
        s_c = jnp.where(cnt_c > 0, s_c - slope * delta_c.astype(F32), NEG)
        s_n = jnp.where(cnt_n > 0, s_n - slope * delta_n.astype(F32), NEG)
        m = jnp.maximum(jnp.max(s_c, axis=-1, keepdims=True), jnp.max(s_n, axis=-1, keepdims=True))
        p_c = cnt_c * jnp.exp(s_c - m)
        p_n = cnt_n * jnp.exp(s_n - m)
        lsum = jnp.sum(p_c, axis=-1, keepdims=True) + jnp.sum(p_n, axis=-1, keepdims=True)
        o = (jnp.dot(p_c.astype(BF16), vc, preferred_element_type=F32)
             + jnp.dot(p_n.astype(BF16), vn, preferred_element_type=F32))
        outs.append(o / lsum)
    o_ref[...] = jnp.where(head_mask[0], outs[0], outs[1])

    keep = n_past - t_new
    ko_ref[0:keep, :] = kc_ref[t_new:n_past, :]
    ko_ref[keep:n_past, :] = kn_ref[...]
    vo_ref[0:keep, :] = vc_ref[t_new:n_past, :]
    vo_ref[keep:n_past, :] = vn_ref[...]


def _attn_decode(slopes, q, k, v, cache_k, cache_v, l):
    b, t_new, _ = q.shape
    n_past = cache_k.shape[2]
    new = pl.BlockSpec((None, t_new, LANES), lambda i, j: (i, 0, j))
    past = pl.BlockSpec((None, None, n_past, LANES), lambda i, j: (l, i, 0, j))
    win = pl.BlockSpec((None, n_past, LANES), lambda i, j: (i, 0, j))
    win_shape = jax.ShapeDtypeStruct((b, n_past, ATT_W), F32)
    return pl.pallas_call(
        functools.partial(_attn_decode_kernel, n_past=n_past, t_new=t_new),
        grid=(b, ATT_W // LANES),
        in_specs=[pl.BlockSpec(memory_space=pltpu.SMEM), new, new, new, past, past],
        out_specs=[new, win, win],
        out_shape=[jax.ShapeDtypeStruct(q.shape, F32), win_shape, win_shape],
        compiler_params=_params(("arbitrary", "arbitrary")),
        name="attn_decode",
    )(slopes, q, k, v, cache_k, cache_v)


def _conv_taps(ext_rows, cw_ref, cb_ref, rows, c0, cw):
    acc = jnp.broadcast_to(cb_ref[:, c0:c0 + cw], (rows, cw))
    for k in range(CONV_K):
        acc = acc + cw_ref[k:k + 1, c0:c0 + cw] * ext_rows(k)
    return acc


def _mix_tail(x, o, y, gt, gao_ref, lng_ref, lnb_ref, wout_ref, gpost_ref):
    mu = jnp.mean(y, axis=-1, keepdims=True)
    yc = y - mu
    yn = yc * lax.rsqrt(jnp.mean(yc * yc, axis=-1, keepdims=True) + EPS) * lng_ref[...] + lnb_ref[...]
    ya = _silu(yn).astype(BF16)
    on = _rms(o, gao_ref[...]).astype(BF16)
    mix = (jnp.dot(on, wout_ref[0:ATT_W, :], preferred_element_type=F32)
           + jnp.dot(ya, wout_ref[ATT_W:, :], preferred_element_type=F32))
    return x + gt * _rms(mix, gpost_ref[...])


def _mixout_prompt_kernel(x_ref, o_ref, u_ref, halo_ref, gt_ref, gao_ref, cw_ref, cb_ref, lng_ref, lnb_ref,
                          wout_ref, gpost_ref, out_ref, ext_scr, y_scr, *, tm, tiles_per_seq):
    first = pl.program_id(0) % tiles_per_seq == 0
    ext_scr[0:HALO, :] = jnp.where(first, 0.0, halo_ref[...])
    ext_scr[HALO:HALO + tm, :] = u_ref[...]
    off = HALO - (CONV_K - 1)
    rc, cw = 32, 256
    for r0 in range(0, tm, rc):
        for c0 in range(0, CONV_CH, cw):
            y_scr[r0:r0 + rc, c0:c0 + cw] = _conv_taps(
                lambda k: ext_scr[r0 + off + k:r0 + off + k + rc, c0:c0 + cw], cw_ref, cb_ref, rc, c0, cw)
    gt = _mod_rows(gt_ref, False, tiles_per_seq)
    out_ref[...] = _mix_tail(x_ref[...], o_ref[...], y_scr[...], gt, gao_ref, lng_ref, lnb_ref, wout_ref, gpost_ref)


def _mixout_decode_kernel(x_ref, o_ref, ext_ref, gt_ref, gao_ref, cw_ref, cb_ref, lng_ref, lnb_ref,
                          wout_ref, gpost_ref, out_ref, y_scr, *, n_seq, t_new):
    cw = 256
    for b in range(n_seq):
        for c0 in range(0, CONV_CH, cw):
            y_scr[b * t_new:(b + 1) * t_new, c0:c0 + cw] = _conv_taps(
                lambda k: ext_ref[b, k:k + t_new, c0:c0 + cw], cw_ref, cb_ref, t_new, c0, cw)
    out_ref[...] = _mix_tail(x_ref[...], o_ref[...], y_scr[...], gt_ref[...], gao_ref, lng_ref, lnb_ref,
                             wout_ref, gpost_ref)


def _vec(w):
    return pl.BlockSpec((1, w), lambda i: (0, 0))


def _mixout_prompt(x, o, u, mod, g_attn_out, conv_w, conv_b, ln_g, ln_b, w_out, g_post, l, *, tm, tiles_per_seq):
    m_rows = x.shape[0]
    row = lambda w: pl.BlockSpec((tm, w), lambda i: (i, 0))
    halo = pl.BlockSpec((HALO, CONV_CH), lambda i: (jnp.maximum(i * (tm // HALO) - 1, 0), 0))
    return pl.pallas_call(
        functools.partial(_mixout_prompt_kernel, tm=tm, tiles_per_seq=tiles_per_seq),
        grid=(m_rows // tm,),
        in_specs=[row(D_MODEL), row(ATT_W), row(CONV_CH), halo, _mod_spec(False, tm, l, 2), _vec(ATT_W),
                  pl.BlockSpec((CONV_K, CONV_CH), lambda i: (0, 0)), _vec(CONV_CH), _vec(CONV_CH), _vec(CONV_CH),
                  _resident(w_out.shape), _vec(D_MODEL)],
        out_specs=row(D_MODEL),
        out_shape=jax.ShapeDtypeStruct(x.shape, F32),
        scratch_shapes=[pltpu.VMEM((HALO + tm, CONV_CH), F32), pltpu.VMEM((tm, CONV_CH), F32)],
        compiler_params=_params(("arbitrary",)),
        name="mixout_prompt",
    )(x, o, u, u, mod, g_attn_out, conv_w, conv_b, ln_g, ln_b, w_out, g_post)


def _mixout_decode(x, o, ext, mod_rows, g_attn_out, conv_w, conv_b, ln_g, ln_b, w_out, g_post, l):
    m_rows = x.shape[0]
    n_seq, ext_len, _ = ext.shape
    t_new = ext_len - (CONV_K - 1)
    row = lambda w: pl.BlockSpec((m_rows, w), lambda i: (0, 0))
    return pl.pallas_call(
        functools.partial(_mixout_decode_kernel, n_seq=n_seq, t_new=t_new),
        grid=(1,),
        in_specs=[row(D_MODEL), row(ATT_W), pl.BlockSpec(ext.shape, lambda i: (0, 0, 0)),
                  _mod_spec(True, m_rows, l, 2), _vec(ATT_W),
                  pl.BlockSpec((CONV_K, CONV_CH), lambda i: (0, 0)), _vec(CONV_CH), _vec(CONV_CH), _vec(CONV_CH),
                  _resident(w_out.shape), _vec(D_MODEL)],
        out_specs=row(D_MODEL),
        out_shape=jax.ShapeDtypeStruct(x.shape, F32),
        scratch_shapes=[pltpu.VMEM((m_rows, CONV_CH), F32)],
        compiler_params=_params(("arbitrary",)),
        name="mixout_decode",
    )(x, o, ext, mod_rows, g_attn_out, conv_w, conv_b, ln_g, ln_b, w_out, g_post)


def _ffn_kernel(x_ref, sc_ref, sh_ref, gt_ref, gpre_ref, gpost_ref, wg_ref, wu_ref, wd_ref, out_ref, h_scr, acc_scr,
                *, per_row, tiles_per_seq):
    f = pl.program_id(1)

    @pl.when(f == 0)
    def _():
        sc = _mod_rows(sc_ref, per_row, tiles_per_seq)
        sh = _mod_rows(sh_ref, per_row, tiles_per_seq)
        h_scr[...] = (_rms(x_ref[...], gpre_ref[...]) * (1.0 + sc) + sh).astype(BF16)
        acc_scr[...] = jnp.zeros_like(acc_scr)

    h = h_scr[...]
    gate = jnp.dot(h, wg_ref[...], preferred_element_type=F32)
    up = jnp.dot(h, wu_ref[...], preferred_element_type=F32)
    act = (_silu(gate) * up).astype(BF16)
    acc_scr[...] += jnp.dot(act, wd_ref[...], preferred_element_type=F32)

    @pl.when(f == pl.num_programs(1) - 1)
    def _():
        gt = _mod_rows(gt_ref, per_row, tiles_per_seq)
        out_ref[...] = x_ref[...] + gt * _rms(acc_scr[...], gpost_ref[...])


def _ffn(x, mod, g_pre, g_post, w_gu, w_down, l, *, tm, tf, per_row, tiles_per_seq):
    m_rows = x.shape[0]
    nf = D_FF // tf
    row = pl.BlockSpec((tm, D_MODEL), lambda i, f: (i, 0))
    vec = pl.BlockSpec((1, D_MODEL), lambda i, f: (0, 0))
    return pl.pallas_call(
        functools.partial(_ffn_kernel, per_row=per_row, tiles_per_seq=tiles_per_seq),
        grid=(m_rows // tm, nf),
        in_specs=[row, _mod_spec(per_row, tm, l, 4), _mod_spec(per_row, tm, l, 3), _mod_spec(per_row, tm, l, 5),
                  vec, vec,
                  pl.BlockSpec((D_MODEL, tf), lambda i, f: (0, f)),
                  pl.BlockSpec((D_MODEL, tf), lambda i, f: (0, f + nf)),
                  pl.BlockSpec((tf, D_MODEL), lambda i, f: (f, 0))],
        out_specs=row,
        out_shape=jax.ShapeDtypeStruct(x.shape, F32),
        scratch_shapes=[pltpu.VMEM((tm, D_MODEL), BF16), pltpu.VMEM((tm, D_MODEL), F32)],
        compiler_params=_params(("arbitrary", "arbitrary")),
        name="ffn",
    )(x, mod, mod, mod, g_pre, g_post, w_gu, w_gu, w_down)


def kernel(x_prompt, x_sample, c_prompt, c_sample, cache_win_k, cache_win_v, state_conv, g_pre_mix, g_post_mix, g_pre_ffn, g_post_ffn, w_mod, b_mod, w_in, g_attn_out, conv_w, conv_b, conv_ln_g, conv_ln_b, w_out, w_gu, w_down):
    depth = w_mod.shape[0]
    bp, seq, _ = x_prompt.shape
    bd, t_new, _ = x_sample.shape
    n_past = cache_win_k.shape[2]
    assert bp + bd <= MOD_ROWS and seq % (STEPS * BRANCHES[-1][1]) == 0 and n_past >= BRANCHES[-1][0]

    slopes = jnp.asarray(_alibi_slopes(N_HEADS))
    c_all = jnp.zeros((MOD_ROWS, D_MODEL), F32).at[:bp].set(c_prompt).at[bp:bp + bd].set(c_sample)
    mod = _modulation(c_all, w_mod, b_mod)
    mod_dec = jnp.repeat(mod[:, :, bp:bp + bd], t_new, axis=2)

    w_in_b, w_out_b, w_gu_b, w_down_b = (w.astype(BF16) for w in (w_in, w_out, w_gu, w_down))
    cache_k = cache_win_k.reshape(depth, bd, n_past, ATT_W)
    cache_v = cache_win_v.reshape(depth, bd, n_past, ATT_W)

    tm_p = 256
    tps = seq // tm_p
    tm_f = 512
    xp = x_prompt.reshape(bp * seq, D_MODEL)
    xs = x_sample.reshape(bd * t_new, D_MODEL)
    md = bd * t_new
    kp_l, vp_l, cp_l, ks_l, vs_l, cs_l = [], [], [], [], [], []
    for l in range(depth):
        vecs = dict(g_attn_out=g_attn_out[l][None], conv_w=conv_w[l], conv_b=conv_b[l][None],
                    ln_g=conv_ln_g[l][None], ln_b=conv_ln_b[l][None], w_out=w_out_b[l], g_post=g_post_mix[l][None])
        q, k, v, u = _inproj(xp, mod, g_pre_mix[l][None], w_in_b[l], l, tm=tm_p, per_row=False, tiles_per_seq=tps)
        o = _attn_prompt(slopes, q.reshape(bp, seq, ATT_W), k.reshape(bp, seq, ATT_W), v.reshape(bp, seq, ATT_W))
        xp = _mixout_prompt(xp, o.reshape(bp * seq, ATT_W), u, mod, l=l, tm=tm_p, tiles_per_seq=tps, **vecs)
        xp = _ffn(xp, mod, g_pre_ffn[l][None], g_post_ffn[l][None], w_gu_b[l], w_down_b[l], l,
                  tm=tm_f, tf=512, per_row=False, tiles_per_seq=seq // tm_f)
        keep = min(BRANCHES[-1][0], seq)
        kp_l.append(k.reshape(bp, seq, N_HEADS, HEAD_DIM)[:, seq - keep:])
        vp_l.append(v.reshape(bp, seq, N_HEADS, HEAD_DIM)[:, seq - keep:])
        cp_l.append(u.reshape(bp, seq, CONV_CH)[:, seq - (CONV_K - 1):])
        q, k, v, u = _inproj(xs, mod_dec, g_pre_mix[l][None], w_in_b[l], l, tm=md, per_row=True, tiles_per_seq=1)
        o, kn, vn = _attn_decode(slopes, q.reshape(bd, t_new, ATT_W), k.reshape(bd, t_new, ATT_W),
                                 v.reshape(bd, t_new, ATT_W), cache_k, cache_v, l)
        ext = jnp.concatenate([state_conv[l], u.reshape(bd, t_new, CONV_CH)], axis=1)
        xs = _mixout_decode(xs, o.reshape(md, ATT_W), ext, mod_dec, l=l, **vecs)
        xs = _ffn(xs, mod_dec, g_pre_ffn[l][None], g_post_ffn[l][None], w_gu_b[l], w_down_b[l], l,
                  tm=md, tf=512, per_row=True, tiles_per_seq=1)
        ks_l.append(kn.reshape(bd, n_past, N_HEADS, HEAD_DIM))
        vs_l.append(vn.reshape(bd, n_past, N_HEADS, HEAD_DIM))
        cs_l.append(ext[:, t_new:])
    return (xp.reshape(bp, seq, D_MODEL), xs.reshape(bd, t_new, D_MODEL),
            jnp.stack(kp_l), jnp.stack(vp_l), jnp.stack(cp_l),
            jnp.stack(ks_l), jnp.stack(vs_l), jnp.stack(cs_l))
```

```python
import functools
import math

import numpy as np
import jax
import jax.numpy as jnp
from jax import lax
from jax.experimental import pallas as pl
from jax.experimental.pallas import tpu as pltpu

D_MODEL = 2048
ATT_W = 1024
HEAD_DIM = 64
N_HEADS = 16
CONV_CH = 1024
CONV_K = 31
D_FF = 5632
N_MOD = 6
EPS = 1e-6
ATTN_SCALE = HEAD_DIM ** -0.5
NEG = -1e30
BRANCHES = ((128, 1), (512, 4), (2048, 16))
STEPS = 128
LANES = 128
SUBLANES = 8
HALO = 32
MOD_ROWS = 16
VMEM_LIMIT = 56 * 1024 * 1024

F32 = jnp.float32
BF16 = jnp.bfloat16
NT = (((1,), (1,)), ((), ()))


def _alibi_slopes(n):
    def pow2(m):
        start = 2.0 ** (-8.0 / m)
        return [start ** (i + 1) for i in range(m)]
    if math.log2(n).is_integer():
        s = pow2(n)
    else:
        m = 2 ** math.floor(math.log2(n))
        s = pow2(m) + pow2(2 * m)[0::2][: n - m]
    return np.array(s, dtype=np.float32)


def _rms(x, g):
    return x * lax.rsqrt(jnp.mean(x * x, axis=-1, keepdims=True) + EPS) * g


def _silu(x):
    return x * jax.nn.sigmoid(x)


def _params(sem):
    return pltpu.CompilerParams(dimension_semantics=sem, vmem_limit_bytes=VMEM_LIMIT)


def _layer_resident(shape, l):
    nd = len(shape) - 1
    return pl.BlockSpec((None,) + tuple(shape[1:]), lambda *_: (l,) + (0,) * nd, pipeline_mode=pl.Buffered(1))


def _layer_vec(w, l):
    return pl.BlockSpec((None, 1, w), lambda *_: (l, 0, 0))


def _mod_kernel(c_ref, w_ref, b_ref, o_ref):
    s = _silu(c_ref[...]).astype(BF16)
    o_ref[...] = jnp.dot(s, w_ref[...].astype(BF16), preferred_element_type=F32) + b_ref[...]


def _modulation(c_all, w_mod, b_mod):
    depth = w_mod.shape[0]
    tn = 1024
    nj = D_MODEL // tn
    return pl.pallas_call(
        _mod_kernel,
        grid=(depth, N_MOD, nj),
        in_specs=[
            pl.BlockSpec((MOD_ROWS, D_MODEL), lambda l, m, j: (0, 0)),
            pl.BlockSpec((None, D_MODEL, tn), lambda l, m, j: (l, 0, m * nj + j)),
            pl.BlockSpec((None, 1, tn), lambda l, m, j: (l, 0, m * nj + j)),
        ],
        out_specs=pl.BlockSpec((None, None, MOD_ROWS, tn), lambda l, m, j: (l, m, 0, j)),
        out_shape=jax.ShapeDtypeStruct((depth, N_MOD, MOD_ROWS, D_MODEL), F32),
        compiler_params=_params(("arbitrary", "arbitrary", "arbitrary")),
        name="modulation",
    )(c_all, w_mod, b_mod.reshape(depth, 1, N_MOD * D_MODEL))


def _mod_rows(ref, per_row, tiles_per_seq):
    if per_row:
        return ref[...]
    return ref[pl.ds(pl.program_id(0) // tiles_per_seq, 1), :]


def _mod_spec(per_row, tm, l, m):
    if per_row:
        return pl.BlockSpec((None, None, tm, D_MODEL), lambda i, *_: (l, m, i, 0))
    return pl.BlockSpec((None, None, MOD_ROWS, D_MODEL), lambda i, *_: (l, m, 0, 0))


def _inproj_kernel(x_ref, sc_ref, sh_ref, g_ref, w_ref, q_ref, k_ref, v_ref, u_ref, *, per_row, tiles_per_seq):
    sc = _mod_rows(sc_ref, per_row, tiles_per_seq)
    sh = _mod_rows(sh_ref, per_row, tiles_per_seq)
    h = (_rms(x_ref[...], g_ref[...]) * (1.0 + sc) + sh).astype(BF16)

    def proj(c):
        return jnp.dot(h, w_ref[:, c * ATT_W:(c + 1) * ATT_W], preferred_element_type=F32)

    q_ref[...] = proj(0)
    k_ref[...] = proj(1)
    v_ref[...] = proj(2)
    u_ref[...] = proj(3) * jax.nn.sigmoid(proj(4))


def _inproj(x, mod, g_pre, w_in, l, *, tm, per_row, tiles_per_seq):
    m_rows = x.shape[0]
    row = lambda w: pl.BlockSpec((tm, w), lambda i: (i, 0))
    out = jax.ShapeDtypeStruct((m_rows, ATT_W), F32)
    return pl.pallas_call(
        functools.partial(_inproj_kernel, per_row=per_row, tiles_per_seq=tiles_per_seq),
        grid=(m_rows // tm,),
        in_specs=[
            row(D_MODEL),
            _mod_spec(per_row, tm, l, 1),
            _mod_spec(per_row, tm, l, 0),
            _layer_vec(D_MODEL, l),
            _layer_resident(w_in.shape, l),
        ],
        out_specs=[row(ATT_W)] * 4,
        out_shape=[out] * 4,
        compiler_params=_params(("arbitrary",)),
        name="inproj",
    )(x, mod, mod, g_pre, w_in)


def _attn_prompt_kernel(slopes_ref, q_ref, k_ref, v_ref, *rest, seq, chained):
    o_ref, kt_ref, vt_ref, x4_scr, res_scr, bias_scr = rest[2:] if chained else rest
    hp = pl.program_id(1)
    lane = lax.broadcasted_iota(jnp.int32, (1, LANES), 1)
    head0 = lane < HEAD_DIM
    plane = seq // 4

    qi = lax.broadcasted_iota(jnp.int32, (STEPS, 2 * STEPS), 0)
    ki = lax.broadcasted_iota(jnp.int32, (STEPS, 2 * STEPS), 1)
    j = STEPS + qi - ki
    valid = (j >= 0) & (j <= STEPS)
    jf = j.astype(F32)
    for h in range(2):
        slope = slopes_ref[2 * hp + h]
        for bi, (_, d) in enumerate(BRANCHES):
            bias_scr[bi, h * STEPS:(h + 1) * STEPS, :] = jnp.where(valid, -(slope * d) * jf, NEG)

    for c, ref in enumerate((q_ref, k_ref, v_ref)):
        for r in range(4):
            x = ref[pl.ds(r, plane, stride=4), :]
            x4_scr[c, r * plane:(r + 1) * plane, :] = x * ATTN_SCALE if c == 0 else x
    for c in range(seq // LANES):
        cols = slice(c * LANES, (c + 1) * LANES)
        kt_ref[:, cols] = k_ref[cols, :].T
        vt_ref[:, cols] = v_ref[cols, :].T

    def attend(q, k_parts, v_parts, bi):
        nk = STEPS * len(k_parts)
        kc = (jnp.concatenate(k_parts, axis=0) if len(k_parts) > 1 else k_parts[0]).astype(BF16)
        vc = (jnp.concatenate(v_parts, axis=0) if len(v_parts) > 1 else v_parts[0]).astype(BF16)
        vc1 = jnp.concatenate([vc, jnp.ones((nk, LANES), BF16)], axis=1)
        q2 = jnp.concatenate([jnp.where(head0, q, 0.0), jnp.where(head0, 0.0, q)], axis=0).astype(BF16)
        s = lax.dot_general(q2, kc, NT, preferred_element_type=F32)
        s = s + (bias_scr[bi] if nk == 2 * STEPS else bias_scr[bi, :, STEPS:])
        m = jnp.max(s, axis=-1, keepdims=True)
        p = jnp.exp(s - m).astype(BF16)
        r = jnp.dot(p, vc1, preferred_element_type=F32)
        num = jnp.where(head0, r[:STEPS, :LANES], r[STEPS:, :LANES])
        den = jnp.where(head0, r[:STEPS, LANES:], r[STEPS:, LANES:])
        mx = jnp.where(head0, m[:STEPS], m[STEPS:])
        return num * (1.0 / den), mx + jnp.log(den)

    def block1(start, first):
        own = pl.ds(start, STEPS)
        q = q_ref[own, :] * ATTN_SCALE
        if first:
            o, lse = attend(q, [k_ref[own, :]], [v_ref[own, :]], 0)
        else:
            prev = pl.ds(pl.multiple_of(start - STEPS, STEPS), STEPS)
            o, lse = attend(q, [k_ref[prev, :], k_ref[own, :]], [v_ref[prev, :], v_ref[own, :]], 0)
        res_scr[0, own, :] = o
        res_scr[1, own, :] = lse

    group = 5
    nb1 = seq // STEPS
    block1(0, True)

    def body1(g, carry):
        for i in range(group):
            block1(pl.multiple_of((1 + g * group + i) * STEPS, STEPS), False)
        return carry
    lax.fori_loop(0, (nb1 - 1) // group, body1, 0)

    def block4(base, first):
        own = pl.ds(base, STEPS)
        q = x4_scr[0, own, :]
        if first:
            o, lse = attend(q, [x4_scr[1, own, :]], [x4_scr[2, own, :]], 1)
        else:
            prev = pl.ds(pl.multiple_of(base - STEPS, STEPS), STEPS)
            o, lse = attend(q, [x4_scr[1, prev, :], x4_scr[1, own, :]], [x4_scr[2, prev, :], x4_scr[2, own, :]], 1)
        res_scr[2, own, :] = o
        res_scr[3, own, :] = lse

    for r in range(4):
        block4(r * plane, True)

    def body4(n, carry):
        for r in range(4):
            block4(pl.multiple_of(r * plane + n * STEPS, STEPS), False)
        return carry
    lax.fori_loop(1, plane // STEPS, body4, 0)

    def body16(r4, carry):
        for rp in range(4):
            rows = pl.ds(r4 * plane + rp, STEPS, stride=4)
            o, lse = attend(x4_scr[0, rows, :], [x4_scr[1, rows, :]], [x4_scr[2, rows, :]], 2)
            res_scr[4, rows, :] = o
            res_scr[5, rows, :] = lse
        return carry
    lax.fori_loop(0, 4, body16, 0)

    def merge(r4, carry):
        for c in range(plane // STEPS):
            mine = pl.ds(pl.multiple_of(r4 * plane + c * STEPS, STEPS), STEPS)
            nat = pl.ds(r4 + 4 * c * STEPS, STEPS, stride=4)
            l0, l1, l2 = res_scr[1, nat, :], res_scr[3, mine, :], res_scr[5, mine, :]
            m = jnp.maximum(jnp.maximum(l0, l1), l2)
            e0, e1, e2 = jnp.exp(l0 - m), jnp.exp(l1 - m), jnp.exp(l2 - m)
            num = e0 * res_scr[0, nat, :] + e1 * res_scr[2, mine, :] + e2 * res_scr[4, mine, :]
            o_ref[nat, :] = num * (1.0 / (e0 + e1 + e2))
        return carry
    lax.fori_loop(0, 4, merge, 0)


def _attn_prompt(slopes, q, k, v, win_k, win_v, l, depth):
    b, seq, _ = q.shape
    chained = win_k is not None
    blk = pl.BlockSpec((None, seq, LANES), lambda i, j: (i, 0, j))
    win = pl.BlockSpec((None, None, LANES, seq), lambda i, j: (l, i, j, 0))
    win_shape = jax.ShapeDtypeStruct((depth, b, ATT_W, seq), F32)
    any_spec = pl.BlockSpec(memory_space=pl.ANY)
    return pl.pallas_call(
        functools.partial(_attn_prompt_kernel, seq=seq, chained=chained),
        grid=(b, ATT_W // LANES),
        in_specs=[pl.BlockSpec(memory_space=pltpu.SMEM), blk, blk, blk] + ([any_spec, any_spec] if chained else []),
        out_specs=[blk, win, win],
        out_shape=[jax.ShapeDtypeStruct(q.shape, F32), win_shape, win_shape],
        input_output_aliases={4: 1, 5: 2} if chained else {},
        scratch_shapes=[
            pltpu.VMEM((3, seq, LANES), F32),
            pltpu.VMEM((6, seq, LANES), F32),
            pltpu.VMEM((3, 2 * STEPS, 2 * STEPS), F32),
        ],
        compiler_params=_params(("arbitrary", "arbitrary")),
        name="attn_prompt",
    )(slopes, q, k, v, *((win_k, win_v) if chained else ()))


def _branch_count(delta):
    cnt = jnp.zeros(delta.shape, F32)
    for w, d in BRANCHES:
        ok = (delta >= 0) & (delta <= w) & ((delta & (d - 1)) == 0)
        cnt = cnt + ok.astype(F32)
    return cnt


def _attn_decode_kernel(slopes_ref, q_ref, kn_ref, vn_ref, kc_ref, vc_ref, *rest, n_past, t_new, chained):
    o_ref, ko_ref, vo_ref = rest[2:] if chained else rest
    hp = pl.program_id(1)
    lane = lax.broadcasted_iota(jnp.int32, (1, LANES), 1)
    head0 = lane < HEAD_DIM
    keep = n_past - t_new

    def shifted(cache_ref, new_ref, out_ref):
        rolled = pltpu.roll(cache_ref[...], n_past - t_new, axis=1)
        new_t = jnp.concatenate([new_ref[...], jnp.zeros((LANES - t_new, LANES), F32)], axis=0).T
        new_t = pltpu.roll(new_t, LANES - t_new, axis=1)
        out_ref[:, :n_past - LANES] = rolled[:, :n_past - LANES]
        out_ref[:, n_past - LANES:] = jnp.where(lane >= LANES - t_new, new_t, rolled[:, n_past - LANES:])
    shifted(kc_ref, kn_ref, ko_ref)
    shifted(vc_ref, vn_ref, vo_ref)

    rows = 2 * t_new
    ti = lax.broadcasted_iota(jnp.int32, (rows, n_past), 0) % t_new
    ci = lax.broadcasted_iota(jnp.int32, (rows, n_past), 1)
    delta_c = n_past + ti - ci
    cnt_c = _branch_count(delta_c)
    delta_w = ti - (ci - keep)
    cnt_w = jnp.where(ci >= keep, _branch_count(delta_w), 0.0)
    row = lax.broadcasted_iota(jnp.int32, (rows, 1), 0)
    slope = jnp.where(row < t_new, slopes_ref[2 * hp], slopes_ref[2 * hp + 1])

    q = q_ref[...] * ATTN_SCALE
    q2 = jnp.concatenate([jnp.where(head0, q, 0.0), jnp.where(head0, 0.0, q)], axis=0).astype(BF16)
    s_c = jnp.dot(q2, kc_ref[...].astype(BF16), preferred_element_type=F32)
    s_w = jnp.dot(q2, ko_ref[...].astype(BF16), preferred_element_type=F32)
    s_c = jnp.where(cnt_c > 0, s_c - slope * delta_c.astype(F32), NEG)
    s_w = jnp.where(cnt_w > 0, s_w - slope * delta_w.astype(F32), NEG)
    m = jnp.maximum(jnp.max(s_c, axis=-1, keepdims=True), jnp.max(s_w, axis=-1, keepdims=True))
    p_c = cnt_c * jnp.exp(s_c - m)
    p_w = cnt_w * jnp.exp(s_w - m)
    den = jnp.sum(p_c, axis=-1, keepdims=True) + jnp.sum(p_w, axis=-1, keepdims=True)
    num = (lax.dot_general(p_c.astype(BF16), vc_ref[...].astype(BF16), NT, preferred_element_type=F32)
           + lax.dot_general(p_w.astype(BF16), vo_ref[...].astype(BF16), NT, preferred_element_type=F32))
    o = num * (1.0 / den)
    o_ref[...] = jnp.where(head0, o[:t_new], o[t_new:])


def _attn_decode(slopes, q, k, v, cache_k, cache_v, win_k, win_v, l):
    b, t_new, _ = q.shape
    depth, _, _, n_past = cache_k.shape
    chained = win_k is not None
    new = pl.BlockSpec((None, t_new, LANES), lambda i, j: (i, 0, j))
    win = pl.BlockSpec((None, None, LANES, n_past), lambda i, j: (l, i, j, 0))
    win_shape = jax.ShapeDtypeStruct(cache_k.shape, F32)
    any_spec = pl.BlockSpec(memory_space=pl.ANY)
    return pl.pallas_call(
        functools.partial(_attn_decode_kernel, n_past=n_past, t_new=t_new, chained=chained),
        grid=(b, ATT_W // LANES),
        in_specs=[pl.BlockSpec(memory_space=pltpu.SMEM), new, new, new, win, win]
        + ([any_spec, any_spec] if chained else []),
        out_specs=[new, win, win],
        out_shape=[jax.ShapeDtypeStruct(q.shape, F32), win_shape, win_shape],
        input_output_aliases={6: 1, 7: 2} if chained else {},
        compiler_params=_params(("arbitrary", "arbitrary")),
        name="attn_decode",
    )(slopes, q, k, v, cache_k, cache_v, *((win_k, win_v) if chained else ()))


def _mix_tail(x, o, y, gt, gao_ref, cb_ref, lng_ref, lnb_ref, wout_ref, gpost_ref):
    y = y + cb_ref[...]
    mu = jnp.mean(y, axis=-1, keepdims=True)
    yc = y - mu
    yn = yc * lax.rsqrt(jnp.mean(yc * yc, axis=-1, keepdims=True) + EPS) * lng_ref[...] + lnb_ref[...]
    ya = _silu(yn).astype(BF16)
    on = _rms(o, gao_ref[...]).astype(BF16)
    mix = (jnp.dot(on, wout_ref[0:ATT_W, :], preferred_element_type=F32)
           + jnp.dot(ya, wout_ref[ATT_W:, :], preferred_element_type=F32))
    return x + gt * _rms(mix, gpost_ref[...])


def _mixout_prompt_kernel(x_ref, o_ref, u_ref, halo_ref, gt_ref, gao_ref, cw_ref, cb_ref, lng_ref, lnb_ref,
                          wout_ref, gpost_ref, out_ref, ext_scr, sh_scr, y_scr, *, tm, tiles_per_seq):
    first = pl.program_id(0) % tiles_per_seq == 0
    ext_scr[0:HALO, :] = jnp.where(first, 0.0, halo_ref[...])
    ext_scr[HALO:HALO + tm, :] = u_ref[...]
    sh_rows = sh_scr.shape[1]
    for b in range(1, SUBLANES):
        sh_scr[b - 1] = ext_scr[b:b + sh_rows, :]
    off = HALO - (CONV_K - 1)
    rc, cw = 32, 256
    for r0 in range(0, tm, rc):
        for c0 in range(0, CONV_CH, cw):
            acc = None
            for k in range(CONV_K):
                a, b = divmod(off + k, SUBLANES)
                rows = slice(r0 + SUBLANES * a, r0 + SUBLANES * a + rc)
                src = ext_scr[rows, c0:c0 + cw] if b == 0 else sh_scr[b - 1, rows, c0:c0 + cw]
                term = cw_ref[k:k + 1, c0:c0 + cw] * src
                acc = term if acc is None else acc + term
            y_scr[r0:r0 + rc, c0:c0 + cw] = acc
    gt = _mod_rows(gt_ref, False, tiles_per_seq)
    out_ref[...] = _mix_tail(x_ref[...], o_ref[...], y_scr[...], gt, gao_ref, cb_ref, lng_ref, lnb_ref,
                             wout_ref, gpost_ref)


def _mixout_decode_kernel(x_ref, o_ref, ext_ref, gt_ref, gao_ref, cw_ref, cb_ref, lng_ref, lnb_ref,
                          wout_ref, gpost_ref, out_ref, y_scr, *, n_seq, t_new):
    cw = 256
    for b in range(n_seq):
        for c0 in range(0, CONV_CH, cw):
            acc = None
            for k in range(CONV_K):
                term = cw_ref[k:k + 1, c0:c0 + cw] * ext_ref[b, k:k + t_new, c0:c0 + cw]
                acc = term if acc is None else acc + term
            y_scr[b * t_new:(b + 1) * t_new, c0:c0 + cw] = acc
    out_ref[...] = _mix_tail(x_ref[...], o_ref[...], y_scr[...], gt_ref[...], gao_ref, cb_ref, lng_ref, lnb_ref,
                             wout_ref, gpost_ref)


def _mixout_prompt(x, o, u, mod, g_attn_out, conv_w, conv_b, ln_g, ln_b, w_out, g_post, l, *, tm, tiles_per_seq):
    m_rows = x.shape[0]
    row = lambda w: pl.BlockSpec((tm, w), lambda i: (i, 0))
    halo = pl.BlockSpec((HALO, CONV_CH), lambda i: (jnp.maximum(i * (tm // HALO) - 1, 0), 0))
    sh_rows = tm + HALO - SUBLANES
    return pl.pallas_call(
        functools.partial(_mixout_prompt_kernel, tm=tm, tiles_per_seq=tiles_per_seq),
        grid=(m_rows // tm,),
        in_specs=[row(D_MODEL), row(ATT_W), row(CONV_CH), halo, _mod_spec(False, tm, l, 2), _layer_vec(ATT_W, l),
                  pl.BlockSpec((None, CONV_K, CONV_CH), lambda i: (l, 0, 0)),
                  _layer_vec(CONV_CH, l), _layer_vec(CONV_CH, l), _layer_vec(CONV_CH, l),
                  _layer_resident(w_out.shape, l), _layer_vec(D_MODEL, l)],
        out_specs=row(D_MODEL),
        out_shape=jax.ShapeDtypeStruct(x.shape, F32),
        scratch_shapes=[pltpu.VMEM((HALO + tm, CONV_CH), F32),
                        pltpu.VMEM((SUBLANES - 1, sh_rows, CONV_CH), F32),
                        pltpu.VMEM((tm, CONV_CH), F32)],
        compiler_params=_params(("arbitrary",)),
        name="mixout_prompt",
    )(x, o, u, u, mod, g_attn_out, conv_w, conv_b, ln_g, ln_b, w_out, g_post)


def _mixout_decode(x, o, ext, mod_rows, g_attn_out, conv_w, conv_b, ln_g, ln_b, w_out, g_post, l):
    m_rows = x.shape[0]
    n_seq, ext_len, _ = ext.shape
    t_new = ext_len - (CONV_K - 1)
    row = lambda w: pl.BlockSpec((m_rows, w), lambda i: (0, 0))
    return pl.pallas_call(
        functools.partial(_mixout_decode_kernel, n_seq=n_seq, t_new=t_new),
        grid=(1,),
        in_specs=[row(D_MODEL), row(ATT_W), pl.BlockSpec(ext.shape, lambda i: (0, 0, 0)),
                  _mod_spec(True, m_rows, l, 2), _layer_vec(ATT_W, l),
                  pl.BlockSpec((None, CONV_K, CONV_CH), lambda i: (l, 0, 0)),
                  _layer_vec(CONV_CH, l), _layer_vec(CONV_CH, l), _layer_vec(CONV_CH, l),
                  _layer_resident(w_out.shape, l), _layer_vec(D_MODEL, l)],
        out_specs=row(D_MODEL),
        out_shape=jax.ShapeDtypeStruct(x.shape, F32),
        scratch_shapes=[pltpu.VMEM((m_rows, CONV_CH), F32)],
        compiler_params=_params(("arbitrary",)),
        name="mixout_decode",
    )(x, o, ext, mod_rows, g_attn_out, conv_w, conv_b, ln_g, ln_b, w_out, g_post)


def _ffn_kernel(x_ref, sc_ref, sh_ref, gt_ref, gpre_ref, gpost_ref, wg_ref, wu_ref, wd_ref, out_ref, h_scr, acc_scr,
                *, per_row, tiles_per_seq):
    f = pl.program_id(1)

    @pl.when(f == 0)
    def _():
        sc = _mod_rows(sc_ref, per_row, tiles_per_seq)
        sh = _mod_rows(sh_ref, per_row, tiles_per_seq)
        h_scr[...] = (_rms(x_ref[...], gpre_ref[...]) * (1.0 + sc) + sh).astype(BF16)
        acc_scr[...] = jnp.zeros_like(acc_scr)

    h = h_scr[...]
    gate = jnp.dot(h, wg_ref[...], preferred_element_type=F32)
    up = jnp.dot(h, wu_ref[...], preferred_element_type=F32)
    act = (_silu(gate) * up).astype(BF16)
    acc_scr[...] += jnp.dot(act, wd_ref[...], preferred_element_type=F32)

    @pl.when(f == pl.num_programs(1) - 1)
    def _():
        gt = _mod_rows(gt_ref, per_row, tiles_per_seq)
        out_ref[...] = x_ref[...] + gt * _rms(acc_scr[...], gpost_ref[...])


def _ffn(x, mod, g_pre, g_post, w_gu, w_down, l, *, tm, tf, per_row, tiles_per_seq):
    m_rows = x.shape[0]
    nf = D_FF // tf
    row = pl.BlockSpec((tm, D_MODEL), lambda i, f: (i, 0))
    return pl.pallas_call(
        functools.partial(_ffn_kernel, per_row=per_row, tiles_per_seq=tiles_per_seq),
        grid=(m_rows // tm, nf),
        in_specs=[row, _mod_spec(per_row, tm, l, 4), _mod_spec(per_row, tm, l, 3), _mod_spec(per_row, tm, l, 5),
                  _layer_vec(D_MODEL, l), _layer_vec(D_MODEL, l),
                  pl.BlockSpec((None, D_MODEL, tf), lambda i, f: (l, 0, f)),
                  pl.BlockSpec((None, D_MODEL, tf), lambda i, f: (l, 0, f + nf)),
                  pl.BlockSpec((None, tf, D_MODEL), lambda i, f: (l, f, 0))],
        out_specs=row,
        out_shape=jax.ShapeDtypeStruct(x.shape, F32),
        scratch_shapes=[pltpu.VMEM((tm, D_MODEL), BF16), pltpu.VMEM((tm, D_MODEL), F32)],
        compiler_params=_params(("arbitrary", "arbitrary")),
        name="ffn",
    )(x, mod, mod, mod, g_pre, g_post, w_gu, w_gu, w_down)


def _feature_major(win):
    depth, b, p, h, dh = win.shape
    return jnp.transpose(win, (0, 1, 3, 4, 2)).reshape(depth, b, h * dh, p)


def _position_major(win):
    depth, b, w, p = win.shape
    return jnp.transpose(win.reshape(depth, b, N_HEADS, HEAD_DIM, p), (0, 1, 4, 2, 3))


def kernel(x_prompt, x_sample, c_prompt, c_sample, cache_win_k, cache_win_v, state_conv, g_pre_mix, g_post_mix, g_pre_ffn, g_post_ffn, w_mod, b_mod, w_in, g_attn_out, conv_w, conv_b, conv_ln_g, conv_ln_b, w_out, w_gu, w_down):
    depth = w_mod.shape[0]
    bp, seq, _ = x_prompt.shape
    bd, t_new, _ = x_sample.shape
    n_past = cache_win_k.shape[2]
    assert bp + bd <= MOD_ROWS and t_new == SUBLANES
    assert seq == STEPS * BRANCHES[-1][1] and n_past >= BRANCHES[-1][0] and seq >= BRANCHES[-1][0]

    slopes = jnp.asarray(_alibi_slopes(N_HEADS))
    c_all = jnp.zeros((MOD_ROWS, D_MODEL), F32).at[:bp].set(c_prompt).at[bp:bp + bd].set(c_sample)
    mod = _modulation(c_all, w_mod, b_mod)
    mod_dec = jnp.repeat(mod[:, :, bp:bp + bd], t_new, axis=2)

    w_in_b, w_out_b, w_gu_b, w_down_b = (w.astype(BF16) for w in (w_in, w_out, w_gu, w_down))
    vec = lambda p: p.reshape(depth, 1, p.shape[-1])
    g_pre_mix, g_post_mix, g_pre_ffn, g_post_ffn = vec(g_pre_mix), vec(g_post_mix), vec(g_pre_ffn), vec(g_post_ffn)
    mix_params = dict(g_attn_out=vec(g_attn_out), conv_w=conv_w, conv_b=vec(conv_b), ln_g=vec(conv_ln_g),
                      ln_b=vec(conv_ln_b), w_out=w_out_b, g_post=g_post_mix)
    cache_k = _feature_major(cache_win_k)
    cache_v = _feature_major(cache_win_v)

    tm_p = 256
    tps = seq // tm_p
    tm_f = 512
    xp = x_prompt.reshape(bp * seq, D_MODEL)
    xs = x_sample.reshape(bd * t_new, D_MODEL)
    md = bd * t_new
    wkp = wvp = wks = wvs = None
    cp_l, cs_l = [], []
    for l in range(depth):
        q, k, v, u = _inproj(xp, mod, g_pre_mix, w_in_b, l, tm=tm_p, per_row=False, tiles_per_seq=tps)
        o, wkp, wvp = _attn_prompt(slopes, q.reshape(bp, seq, ATT_W), k.reshape(bp, seq, ATT_W),
                                   v.reshape(bp, seq, ATT_W), wkp, wvp, l, depth)
        xp = _mixout_prompt(xp, o.reshape(bp * seq, ATT_W), u, mod, l=l, tm=tm_p, tiles_per_seq=tps, **mix_params)
        xp = _ffn(xp, mod, g_pre_ffn, g_post_ffn, w_gu_b, w_down_b, l,
                  tm=tm_f, tf=512, per_row=False, tiles_per_seq=seq // tm_f)
        cp_l.append(u.reshape(bp, seq, CONV_CH)[:, seq - (CONV_K - 1):])
        q, k, v, u = _inproj(xs, mod_dec, g_pre_mix, w_in_b, l, tm=md, per_row=True, tiles_per_seq=1)
        o, wks, wvs = _attn_decode(slopes, q.reshape(bd, t_new, ATT_W), k.reshape(bd, t_new, ATT_W),
                                   v.reshape(bd, t_new, ATT_W), cache_k, cache_v, wks, wvs, l)
        ext = jnp.concatenate([state_conv[l], u.reshape(bd, t_new, CONV_CH)], axis=1)
        xs = _mixout_decode(xs, o.reshape(md, ATT_W), ext, mod_dec, l=l, **mix_params)
        xs = _ffn(xs, mod_dec, g_pre_ffn, g_post_ffn, w_gu_b, w_down_b, l,
                  tm=md, tf=512, per_row=True, tiles_per_seq=1)
        cs_l.append(ext[:, t_new:])
    return (xp.reshape(bp, seq, D_MODEL), xs.reshape(bd, t_new, D_MODEL),
            _position_major(wkp), _position_major(wvp), jnp.stack(cp_l),
            _position_major(wks), _position_major(wvs), jnp.stack(cs_l))
```

```python
import functools
import math

import numpy as np
import jax
import jax.numpy as jnp
from jax import lax
from jax.experimental import pallas as pl
from jax.experimental.pallas import tpu as pltpu

D_MODEL = 2048
ATT_W = 1024
HEAD_DIM = 64
N_HEADS = 16
CONV_CH = 1024
CONV_K = 31
D_FF = 5632
N_MOD = 6
EPS = 1e-6
ATTN_SCALE = HEAD_DIM ** -0.5
LOG2E = math.log2(math.e)
QK_SCALE = ATTN_SCALE * LOG2E
NEG = -1e30
BRANCHES = ((128, 1), (512, 4), (2048, 16))
STEPS = 128
LANES = 128
SUBLANES = 8
HALO = 32
MOD_ROWS = 16
NORM_ROWS = 16
VMEM_LIMIT = 56 * 1024 * 1024

F32 = jnp.float32
BF16 = jnp.bfloat16
NT = (((1,), (1,)), ((), ()))


def _alibi_slopes(n):
    def pow2(m):
        start = 2.0 ** (-8.0 / m)
        return [start ** (i + 1) for i in range(m)]
    if math.log2(n).is_integer():
        s = pow2(n)
    else:
        m = 2 ** math.floor(math.log2(n))
        s = pow2(m) + pow2(2 * m)[0::2][: n - m]
    return np.array(s, dtype=np.float32)


def _rms(x, g):
    return x * lax.rsqrt(jnp.mean(x * x, axis=-1, keepdims=True) + EPS) * g


def _silu(x):
    return x * jax.nn.sigmoid(x)


def _params(sem):
    return pltpu.CompilerParams(dimension_semantics=sem, vmem_limit_bytes=VMEM_LIMIT)


def _layer_resident(shape, l):
    nd = len(shape) - 1
    return pl.BlockSpec((None,) + tuple(shape[1:]), lambda *_: (l,) + (0,) * nd, pipeline_mode=pl.Buffered(1))


def _layer_vec(w, l):
    return pl.BlockSpec((None, 1, w), lambda *_: (l, 0, 0))


def _mod_kernel(c_ref, w_ref, b_ref, o_ref):
    s = _silu(c_ref[...]).astype(BF16)
    o_ref[...] = jnp.dot(s, w_ref[...].astype(BF16), preferred_element_type=F32) + b_ref[...]


def _modulation(c_all, w_mod, b_mod):
    depth = w_mod.shape[0]
    tn = 1024
    nj = D_MODEL // tn
    return pl.pallas_call(
        _mod_kernel,
        grid=(depth, N_MOD, nj),
        in_specs=[
            pl.BlockSpec((MOD_ROWS, D_MODEL), lambda l, m, j: (0, 0)),
            pl.BlockSpec((None, D_MODEL, tn), lambda l, m, j: (l, 0, m * nj + j)),
            pl.BlockSpec((None, 1, tn), lambda l, m, j: (l, 0, m * nj + j)),
        ],
        out_specs=pl.BlockSpec((None, None, MOD_ROWS, tn), lambda l, m, j: (l, m, 0, j)),
        out_shape=jax.ShapeDtypeStruct((depth, N_MOD, MOD_ROWS, D_MODEL), F32),
        compiler_params=_params(("arbitrary", "arbitrary", "arbitrary")),
        name="modulation",
    )(c_all, w_mod, b_mod.reshape(depth, 1, N_MOD * D_MODEL))


def _mod_rows(ref, per_row, tiles_per_seq):
    if per_row:
        return ref[...]
    return ref[pl.ds(pl.program_id(0) // tiles_per_seq, 1), :]


def _mod_spec(per_row, tm, l, m):
    if per_row:
        return pl.BlockSpec((None, None, tm, D_MODEL), lambda i, *_: (l, m, i, 0))
    return pl.BlockSpec((None, None, MOD_ROWS, D_MODEL), lambda i, *_: (l, m, 0, 0))


def _inproj_kernel(x_ref, sc_ref, sh_ref, g_ref, w_ref, q_ref, k_ref, v_ref, u_ref, *, per_row, tiles_per_seq):
    sc = _mod_rows(sc_ref, per_row, tiles_per_seq)
    sh = _mod_rows(sh_ref, per_row, tiles_per_seq)
    h = (_rms(x_ref[...], g_ref[...]) * (1.0 + sc) + sh).astype(BF16)

    def proj(c):
        return jnp.dot(h, w_ref[:, c * ATT_W:(c + 1) * ATT_W], preferred_element_type=F32)

    q_ref[...] = proj(0)
    k_ref[...] = proj(1)
    v_ref[...] = proj(2)
    u_ref[...] = proj(3) * jax.nn.sigmoid(proj(4))


def _inproj(x, mod, g_pre, w_in, l, *, tm, per_row, tiles_per_seq):
    m_rows = x.shape[0]
    row = lambda w: pl.BlockSpec((tm, w), lambda i: (i, 0))
    out = jax.ShapeDtypeStruct((m_rows, ATT_W), F32)
    return pl.pallas_call(
        functools.partial(_inproj_kernel, per_row=per_row, tiles_per_seq=tiles_per_seq),
        grid=(m_rows // tm,),
        in_specs=[
            row(D_MODEL),
            _mod_spec(per_row, tm, l, 1),
            _mod_spec(per_row, tm, l, 0),
            _layer_vec(D_MODEL, l),
            _layer_resident(w_in.shape, l),
        ],
        out_specs=[row(ATT_W)] * 4,
        out_shape=[out] * 4,
        compiler_params=_params(("arbitrary",)),
        name="inproj",
    )(x, mod, mod, g_pre, w_in)


def _attn_prompt_kernel(slopes_ref, q_ref, k_ref, v_ref, *rest, seq, chained):
    o_ref, kt_ref, vt_ref, x4_scr, res_scr, bias_scr = rest[2:] if chained else rest
    hp = pl.program_id(1)
    lane = lax.broadcasted_iota(jnp.int32, (1, LANES), 1)
    head0 = lane < HEAD_DIM
    plane = seq // 4

    qi = lax.broadcasted_iota(jnp.int32, (STEPS, 2 * STEPS), 0)
    ki = lax.broadcasted_iota(jnp.int32, (STEPS, 2 * STEPS), 1)
    j = STEPS + qi - ki
    valid = (j >= 0) & (j <= STEPS)
    jf = j.astype(F32)
    for h in range(2):
        slope = slopes_ref[2 * hp + h]
        for bi, (_, d) in enumerate(BRANCHES):
            bias_scr[bi, h * STEPS:(h + 1) * STEPS, :] = jnp.where(valid, -(slope * d * LOG2E) * jf, NEG)

    for c, ref in enumerate((q_ref, k_ref, v_ref)):
        for r in range(4):
            x = ref[pl.ds(r, plane, stride=4), :]
            x4_scr[c, r * plane:(r + 1) * plane, :] = x * QK_SCALE if c == 0 else x
    for c in range(seq // LANES):
        cols = slice(c * LANES, (c + 1) * LANES)
        kt_ref[:, cols] = k_ref[cols, :].T
        vt_ref[:, cols] = v_ref[cols, :].T

    def attend(q, k_parts, v_parts, bi):
        nk = STEPS * len(k_parts)
        kc = (jnp.concatenate(k_parts, axis=0) if len(k_parts) > 1 else k_parts[0]).astype(BF16)
        vc = (jnp.concatenate(v_parts, axis=0) if len(v_parts) > 1 else v_parts[0]).astype(BF16)
        vc1 = jnp.concatenate([vc, jnp.ones((nk, LANES), BF16)], axis=1)
        q2 = jnp.concatenate([jnp.where(head0, q, 0.0), jnp.where(head0, 0.0, q)], axis=0).astype(BF16)
        s = lax.dot_general(q2, kc, NT, preferred_element_type=F32)
        s = s + (bias_scr[bi] if nk == 2 * STEPS else bias_scr[bi, :, STEPS:])
        m = jnp.max(s, axis=-1, keepdims=True)
        p = jnp.exp2(s - m).astype(BF16)
        r = jnp.dot(p, vc1, preferred_element_type=F32)
        num = jnp.where(head0, r[:STEPS, :LANES], r[STEPS:, :LANES])
        den = jnp.where(head0, r[:STEPS, LANES:], r[STEPS:, LANES:])
        mx = jnp.where(head0, m[:STEPS], m[STEPS:])
        return num * (1.0 / den), mx + jnp.log2(den)

    def block1(start, first):
        own = pl.ds(start, STEPS)
        q = q_ref[own, :] * QK_SCALE
        if first:
            o, lse = attend(q, [k_ref[own, :]], [v_ref[own, :]], 0)
        else:
            prev = pl.ds(start - STEPS, STEPS)
            o, lse = attend(q, [k_ref[prev, :], k_ref[own, :]], [v_ref[prev, :], v_ref[own, :]], 0)
        res_scr[0, own, :] = o
        res_scr[1, own, :] = lse

    for n in range(seq // STEPS):
        block1(n * STEPS, n == 0)

    def block4(base, first):
        own = pl.ds(base, STEPS)
        q = x4_scr[0, own, :]
        if first:
            o, lse = attend(q, [x4_scr[1, own, :]], [x4_scr[2, own, :]], 1)
        else:
            prev = pl.ds(base - STEPS, STEPS)
            o, lse = attend(q, [x4_scr[1, prev, :], x4_scr[1, own, :]], [x4_scr[2, prev, :], x4_scr[2, own, :]], 1)
        res_scr[2, own, :] = o
        res_scr[3, own, :] = lse

    for n in range(plane // STEPS):
        for r in range(4):
            block4(r * plane + n * STEPS, n == 0)

    for r4 in range(4):
        for rp in range(4):
            rows = pl.ds(r4 * plane + rp, STEPS, stride=4)
            o, lse = attend(x4_scr[0, rows, :], [x4_scr[1, rows, :]], [x4_scr[2, rows, :]], 2)
            res_scr[4, rows, :] = o
            res_scr[5, rows, :] = lse

    def merge(r4, carry):
        for c in range(plane // STEPS):
            mine = pl.ds(pl.multiple_of(r4 * plane + c * STEPS, STEPS), STEPS)
            nat = pl.ds(r4 + 4 * c * STEPS, STEPS, stride=4)
            l0, l1, l2 = res_scr[1, nat, :], res_scr[3, mine, :], res_scr[5, mine, :]
            m = jnp.maximum(jnp.maximum(l0, l1), l2)
            e0, e1, e2 = jnp.exp2(l0 - m), jnp.exp2(l1 - m), jnp.exp2(l2 - m)
            num = e0 * res_scr[0, nat, :] + e1 * res_scr[2, mine, :] + e2 * res_scr[4, mine, :]
            o_ref[nat, :] = num * (1.0 / (e0 + e1 + e2))
        return carry
    lax.fori_loop(0, 4, merge, 0)


def _attn_prompt(slopes, q, k, v, win_k, win_v, l, depth):
    b, seq, _ = q.shape
    chained = win_k is not None
    blk = pl.BlockSpec((None, seq, LANES), lambda i, j: (i, 0, j))
    win = pl.BlockSpec((None, None, LANES, seq), lambda i, j: (l, i, j, 0))
    win_shape = jax.ShapeDtypeStruct((depth, b, ATT_W, seq), F32)
    any_spec = pl.BlockSpec(memory_space=pl.ANY)
    return pl.pallas_call(
        functools.partial(_attn_prompt_kernel, seq=seq, chained=chained),
        grid=(b, ATT_W // LANES),
        in_specs=[pl.BlockSpec(memory_space=pltpu.SMEM), blk, blk, blk] + ([any_spec, any_spec] if chained else []),
        out_specs=[blk, win, win],
        out_shape=[jax.ShapeDtypeStruct(q.shape, F32), win_shape, win_shape],
        input_output_aliases={4: 1, 5: 2} if chained else {},
        scratch_shapes=[
            pltpu.VMEM((3, seq, LANES), F32),
            pltpu.VMEM((6, seq, LANES), F32),
            pltpu.VMEM((3, 2 * STEPS, 2 * STEPS), F32),
        ],
        compiler_params=_params(("arbitrary", "arbitrary")),
        name="attn_prompt",
    )(slopes, q, k, v, *((win_k, win_v) if chained else ()))


def _branch_count(delta):
    cnt = jnp.zeros(delta.shape, F32)
    for w, d in BRANCHES:
        ok = (delta >= 0) & (delta <= w) & ((delta & (d - 1)) == 0)
        cnt = cnt + ok.astype(F32)
    return cnt


def _attn_decode_kernel(slopes_ref, q_ref, kn_ref, vn_ref, kc_ref, vc_ref, *rest, n_past, t_new, chained):
    o_ref, ko_ref, vo_ref = rest[2:] if chained else rest
    hp = pl.program_id(1)
    lane = lax.broadcasted_iota(jnp.int32, (1, LANES), 1)
    head0 = lane < HEAD_DIM
    keep = n_past - t_new

    def shifted(cache_ref, new_ref, out_ref):
        rolled = pltpu.roll(cache_ref[...], n_past - t_new, axis=1)
        new_t = jnp.concatenate([new_ref[...], jnp.zeros((LANES - t_new, LANES), F32)], axis=0).T
        new_t = pltpu.roll(new_t, LANES - t_new, axis=1)
        out_ref[:, :n_past - LANES] = rolled[:, :n_past - LANES]
        out_ref[:, n_past - LANES:] = jnp.where(lane >= LANES - t_new, new_t, rolled[:, n_past - LANES:])
    shifted(kc_ref, kn_ref, ko_ref)
    shifted(vc_ref, vn_ref, vo_ref)

    rows = 2 * t_new
    ti = lax.broadcasted_iota(jnp.int32, (rows, n_past), 0) % t_new
    ci = lax.broadcasted_iota(jnp.int32, (rows, n_past), 1)
    delta_c = n_past + ti - ci
    cnt_c = _branch_count(delta_c)
    delta_w = ti - (ci - keep)
    cnt_w = jnp.where(ci >= keep, _branch_count(delta_w), 0.0)
    row = lax.broadcasted_iota(jnp.int32, (rows, 1), 0)
    slope = jnp.where(row < t_new, slopes_ref[2 * hp], slopes_ref[2 * hp + 1])

    q = q_ref[...] * ATTN_SCALE
    q2 = jnp.concatenate([jnp.where(head0, q, 0.0), jnp.where(head0, 0.0, q)], axis=0).astype(BF16)
    s_c = jnp.dot(q2, kc_ref[...].astype(BF16), preferred_element_type=F32)
    s_w = jnp.dot(q2, ko_ref[...].astype(BF16), preferred_element_type=F32)
    s_c = jnp.where(cnt_c > 0, s_c - slope * delta_c.astype(F32), NEG)
    s_w = jnp.where(cnt_w > 0, s_w - slope * delta_w.astype(F32), NEG)
    m = jnp.maximum(jnp.max(s_c, axis=-1, keepdims=True), jnp.max(s_w, axis=-1, keepdims=True))
    p_c = cnt_c * jnp.exp(s_c - m)
    p_w = cnt_w * jnp.exp(s_w - m)
    den = jnp.sum(p_c, axis=-1, keepdims=True) + jnp.sum(p_w, axis=-1, keepdims=True)
    num = (lax.dot_general(p_c.astype(BF16), vc_ref[...].astype(BF16), NT, preferred_element_type=F32)
           + lax.dot_general(p_w.astype(BF16), vo_ref[...].astype(BF16), NT, preferred_element_type=F32))
    o = num * (1.0 / den)
    o_ref[...] = jnp.where(head0, o[:t_new], o[t_new:])


def _attn_decode(slopes, q, k, v, cache_k, cache_v, win_k, win_v, l):
    b, t_new, _ = q.shape
    depth, _, _, n_past = cache_k.shape
    chained = win_k is not None
    new = pl.BlockSpec((None, t_new, LANES), lambda i, j: (i, 0, j))
    win = pl.BlockSpec((None, None, LANES, n_past), lambda i, j: (l, i, j, 0))
    win_shape = jax.ShapeDtypeStruct(cache_k.shape, F32)
    any_spec = pl.BlockSpec(memory_space=pl.ANY)
    return pl.pallas_call(
        functools.partial(_attn_decode_kernel, n_past=n_past, t_new=t_new, chained=chained),
        grid=(b, ATT_W // LANES),
        in_specs=[pl.BlockSpec(memory_space=pltpu.SMEM), new, new, new, win, win]
        + ([any_spec, any_spec] if chained else []),
        out_specs=[new, win, win],
        out_shape=[jax.ShapeDtypeStruct(q.shape, F32), win_shape, win_shape],
        input_output_aliases={6: 1, 7: 2} if chained else {},
        compiler_params=_params(("arbitrary", "arbitrary")),
        name="attn_decode",
    )(slopes, q, k, v, cache_k, cache_v, *((win_k, win_v) if chained else ()))


def _mix_tail(x, o, y, gt, gao_ref, cb_ref, lng_ref, lnb_ref, wout_ref, gpost_ref):
    y = y + cb_ref[...]
    mu = jnp.mean(y, axis=-1, keepdims=True)
    yc = y - mu
    yn = yc * lax.rsqrt(jnp.mean(yc * yc, axis=-1, keepdims=True) + EPS) * lng_ref[...] + lnb_ref[...]
    ya = _silu(yn).astype(BF16)
    on = _rms(o, gao_ref[...]).astype(BF16)
    mix = (jnp.dot(on, wout_ref[0:ATT_W, :], preferred_element_type=F32)
           + jnp.dot(ya, wout_ref[ATT_W:, :], preferred_element_type=F32))
    return x + gt * _rms(mix, gpost_ref[...])


def _mixout_prompt_kernel(x_ref, o_ref, u_ref, halo_ref, gt_ref, gao_ref, cw_ref, cb_ref, lng_ref, lnb_ref,
                          wout_ref, gpost_ref, out_ref, ext_scr, y_scr, *, tm, tiles_per_seq):
    first = pl.program_id(0) % tiles_per_seq == 0
    n_ct = CONV_CH // LANES
    for c in range(n_ct):
        lanes = slice(c * LANES, (c + 1) * LANES)
        ext_scr[c, 0:HALO, :] = jnp.where(first, 0.0, halo_ref[:, lanes])
        ext_scr[c, HALO:HALO + tm, :] = u_ref[:, lanes]
    off = HALO - (CONV_K - 1)
    half = tm // 2
    for c in range(n_ct):
        lanes = slice(c * LANES, (c + 1) * LANES)
        acc = [None, None]
        for k in range(CONV_K):
            w = jnp.broadcast_to(cw_ref[k:k + 1, lanes], (half, LANES))
            for par in range(2):
                term = w * ext_scr[c, pl.ds(par + off + k, half, stride=2), :]
                acc[par] = term if acc[par] is None else acc[par] + term
        for par in range(2):
            y_scr[c, pl.ds(par, half, stride=2), :] = acc[par]
    y = jnp.concatenate([y_scr[c] for c in range(n_ct)], axis=1)
    gt = _mod_rows(gt_ref, False, tiles_per_seq)
    out_ref[...] = _mix_tail(x_ref[...], o_ref[...], y, gt, gao_ref, cb_ref, lng_ref, lnb_ref, wout_ref, gpost_ref)


def _mixout_decode_kernel(x_ref, o_ref, ext_ref, gt_ref, gao_ref, cw_ref, cb_ref, lng_ref, lnb_ref,
                          wout_ref, gpost_ref, out_ref, y_scr, *, n_seq, t_new):
    cw = 256
    for b in range(n_seq):
        for c0 in range(0, CONV_CH, cw):
            acc = None
            for k in range(CONV_K):
                term = cw_ref[k:k + 1, c0:c0 + cw] * ext_ref[b, k:k + t_new, c0:c0 + cw]
                acc = term if acc is None else acc + term
            y_scr[b * t_new:(b + 1) * t_new, c0:c0 + cw] = acc
    out_ref[...] = _mix_tail(x_ref[...], o_ref[...], y_scr[...], gt_ref[...], gao_ref, cb_ref, lng_ref, lnb_ref,
                             wout_ref, gpost_ref)


def _mixout_prompt(x, o, u, mod, g_attn_out, conv_w, conv_b, ln_g, ln_b, w_out, g_post, l, *, tm, tiles_per_seq):
    m_rows = x.shape[0]
    row = lambda w: pl.BlockSpec((tm, w), lambda i: (i, 0))
    halo = pl.BlockSpec((HALO, CONV_CH), lambda i: (jnp.maximum(i * (tm // HALO) - 1, 0), 0))
    return pl.pallas_call(
        functools.partial(_mixout_prompt_kernel, tm=tm, tiles_per_seq=tiles_per_seq),
        grid=(m_rows // tm,),
        in_specs=[row(D_MODEL), row(ATT_W), row(CONV_CH), halo, _mod_spec(False, tm, l, 2), _layer_vec(ATT_W, l),
                  pl.BlockSpec((None, CONV_K, CONV_CH), lambda i: (l, 0, 0)),
                  _layer_vec(CONV_CH, l), _layer_vec(CONV_CH, l), _layer_vec(CONV_CH, l),
                  _layer_resident(w_out.shape, l), _layer_vec(D_MODEL, l)],
        out_specs=row(D_MODEL),
        out_shape=jax.ShapeDtypeStruct(x.shape, F32),
        scratch_shapes=[pltpu.VMEM((CONV_CH // LANES, HALO + tm, LANES), F32),
                        pltpu.VMEM((CONV_CH // LANES, tm, LANES), F32)],
        compiler_params=_params(("arbitrary",)),
        name="mixout_prompt",
    )(x, o, u, u, mod, g_attn_out, conv_w, conv_b, ln_g, ln_b, w_out, g_post)


def _mixout_decode(x, o, ext, mod_rows, g_attn_out, conv_w, conv_b, ln_g, ln_b, w_out, g_post, l):
    m_rows = x.shape[0]
    n_seq, ext_len, _ = ext.shape
    t_new = ext_len - (CONV_K - 1)
    row = lambda w: pl.BlockSpec((m_rows, w), lambda i: (0, 0))
    return pl.pallas_call(
        functools.partial(_mixout_decode_kernel, n_seq=n_seq, t_new=t_new),
        grid=(1,),
        in_specs=[row(D_MODEL), row(ATT_W), pl.BlockSpec(ext.shape, lambda i: (0, 0, 0)),
                  _mod_spec(True, m_rows, l, 2), _layer_vec(ATT_W, l),
                  pl.BlockSpec((None, CONV_K, CONV_CH), lambda i: (l, 0, 0)),
                  _layer_vec(CONV_CH, l), _layer_vec(CONV_CH, l), _layer_vec(CONV_CH, l),
                  _layer_resident(w_out.shape, l), _layer_vec(D_MODEL, l)],
        out_specs=row(D_MODEL),
        out_shape=jax.ShapeDtypeStruct(x.shape, F32),
        scratch_shapes=[pltpu.VMEM((m_rows, CONV_CH), F32)],
        compiler_params=_params(("arbitrary",)),
        name="mixout_decode",
    )(x, o, ext, mod_rows, g_attn_out, conv_w, conv_b, ln_g, ln_b, w_out, g_post)


def _ffn_kernel(x_ref, sc_ref, sh_ref, gt_ref, gpre_ref, gpost_ref, wg_ref, wu_ref, wd_ref, out_ref, *rest,
                per_row, tiles_per_seq, emit_w):
    h_scr = rest[-1]
    wg, wu, wd = wg_ref[...], wu_ref[...], wd_ref[...]
    if emit_w:
        wg, wu, wd = wg.astype(BF16), wu.astype(BF16), wd.astype(BF16)
        rest[0][...] = wg
        rest[1][...] = wu
        rest[2][...] = wd
    f = pl.program_id(1)
    tm = x_ref.shape[0]

    def by_chunks(fn):
        for c in range(tm // NORM_ROWS):
            rows = slice(c * NORM_ROWS, (c + 1) * NORM_ROWS)
            fn(rows, lambda ref: ref[rows, :] if per_row else _mod_rows(ref, False, tiles_per_seq))

    @pl.when(f == 0)
    def _():
        def chunk(rows, mod):
            x = x_ref[rows, :]
            inv = lax.rsqrt(jnp.mean(x * x, axis=-1, keepdims=True) + EPS)
            h_scr[rows, :] = (x * inv * (gpre_ref[...] * (1.0 + mod(sc_ref))) + mod(sh_ref)).astype(BF16)
            out_ref[rows, :] = jnp.zeros((NORM_ROWS, D_MODEL), F32)
        by_chunks(chunk)

    h = h_scr[...]
    gate = jnp.dot(h, wg, preferred_element_type=F32)
    up = jnp.dot(h, wu, preferred_element_type=F32)
    act = (_silu(gate) * up).astype(BF16)
    out_ref[...] += jnp.dot(act, wd, preferred_element_type=F32)

    @pl.when(f == pl.num_programs(1) - 1)
    def _():
        def chunk(rows, mod):
            acc = out_ref[rows, :]
            inv = lax.rsqrt(jnp.mean(acc * acc, axis=-1, keepdims=True) + EPS)
            out_ref[rows, :] = x_ref[rows, :] + acc * inv * (gpost_ref[...] * mod(gt_ref))
        by_chunks(chunk)


def _ffn(x, mod, g_pre, g_post, weights, l, *, tm, tf, per_row, tiles_per_seq):
    m_rows = x.shape[0]
    nf = D_FF // tf
    emit_w = len(weights) == 2
    row = pl.BlockSpec((tm, D_MODEL), lambda i, f: (i, 0))
    row_in = pl.BlockSpec((tm, D_MODEL), lambda i, f: (i, 0), pipeline_mode=pl.Buffered(1))
    col = pl.BlockSpec((D_MODEL, tf), lambda i, f: (0, f))
    chunk = pl.BlockSpec((tf, D_MODEL), lambda i, f: (f, 0))
    if emit_w:
        assert m_rows == tm
        w_gu, w_down = weights
        w_args = (w_gu, w_gu, w_down)
        w_specs = [pl.BlockSpec((None, D_MODEL, tf), lambda i, f: (l, 0, f)),
                   pl.BlockSpec((None, D_MODEL, tf), lambda i, f: (l, 0, f + nf)),
                   pl.BlockSpec((None, tf, D_MODEL), lambda i, f: (l, f, 0))]
        out_specs = [row, col, col, chunk]
        out_shape = [jax.ShapeDtypeStruct(x.shape, F32), jax.ShapeDtypeStruct((D_MODEL, D_FF), BF16),
                     jax.ShapeDtypeStruct((D_MODEL, D_FF), BF16), jax.ShapeDtypeStruct((D_FF, D_MODEL), BF16)]
    else:
        w_args = weights
        w_specs = [col, col, chunk]
        out_specs = row
        out_shape = jax.ShapeDtypeStruct(x.shape, F32)
    return pl.pallas_call(
        functools.partial(_ffn_kernel, per_row=per_row, tiles_per_seq=tiles_per_seq, emit_w=emit_w),
        grid=(m_rows // tm, nf),
        in_specs=[row_in, _mod_spec(per_row, tm, l, 4), _mod_spec(per_row, tm, l, 3), _mod_spec(per_row, tm, l, 5),
                  _layer_vec(D_MODEL, l), _layer_vec(D_MODEL, l)] + w_specs,
        out_specs=out_specs,
        out_shape=out_shape,
        scratch_shapes=[pltpu.VMEM((tm, D_MODEL), BF16)],
        compiler_params=_params(("arbitrary", "arbitrary")),
        name="ffn",
    )(x, mod, mod, mod, g_pre, g_post, *w_args)


def _feature_major(win):
    depth, b, p, h, dh = win.shape
    return jnp.transpose(win, (0, 1, 3, 4, 2)).reshape(depth, b, h * dh, p)


def _position_major(win):
    depth, b, w, p = win.shape
    return jnp.transpose(win.reshape(depth, b, N_HEADS, HEAD_DIM, p), (0, 1, 4, 2, 3))


def kernel(x_prompt, x_sample, c_prompt, c_sample, cache_win_k, cache_win_v, state_conv, g_pre_mix, g_post_mix, g_pre_ffn, g_post_ffn, w_mod, b_mod, w_in, g_attn_out, conv_w, conv_b, conv_ln_g, conv_ln_b, w_out, w_gu, w_down):
    depth = w_mod.shape[0]
    bp, seq, _ = x_prompt.shape
    bd, t_new, _ = x_sample.shape
    n_past = cache_win_k.shape[2]
    assert bp + bd <= MOD_ROWS and t_new == SUBLANES
    assert seq == STEPS * BRANCHES[-1][1] and n_past >= BRANCHES[-1][0] and seq >= BRANCHES[-1][0]

    slopes = jnp.asarray(_alibi_slopes(N_HEADS))
    c_all = jnp.zeros((MOD_ROWS, D_MODEL), F32).at[:bp].set(c_prompt).at[bp:bp + bd].set(c_sample)
    mod = _modulation(c_all, w_mod, b_mod)
    mod_dec = jnp.repeat(mod[:, :, bp:bp + bd], t_new, axis=2)

    w_in_b, w_out_b = w_in.astype(BF16), w_out.astype(BF16)
    vec = lambda p: p.reshape(depth, 1, p.shape[-1])
    g_pre_mix, g_post_mix, g_pre_ffn, g_post_ffn = vec(g_pre_mix), vec(g_post_mix), vec(g_pre_ffn), vec(g_post_ffn)
    mix_params = dict(g_attn_out=vec(g_attn_out), conv_w=conv_w, conv_b=vec(conv_b), ln_g=vec(conv_ln_g),
                      ln_b=vec(conv_ln_b), w_out=w_out_b, g_post=g_post_mix)
    cache_k = _feature_major(cache_win_k)
    cache_v = _feature_major(cache_win_v)

    tm_i = 512
    tm_p = 256
    tps = seq // tm_p
    tm_f = 1024
    xp = x_prompt.reshape(bp * seq, D_MODEL)
    xs = x_sample.reshape(bd * t_new, D_MODEL)
    md = bd * t_new
    wkp = wvp = wks = wvs = None
    cp_l, cs_l = [], []
    for l in range(depth):
        q, k, v, u = _inproj(xs, mod_dec, g_pre_mix, w_in_b, l, tm=md, per_row=True, tiles_per_seq=1)
        o, wks, wvs = _attn_decode(slopes, q.reshape(bd, t_new, ATT_W), k.reshape(bd, t_new, ATT_W),
                                   v.reshape(bd, t_new, ATT_W), cache_k, cache_v, wks, wvs, l)
        ext = jnp.concatenate([state_conv[l], u.reshape(bd, t_new, CONV_CH)], axis=1)
        xs = _mixout_decode(xs, o.reshape(md, ATT_W), ext, mod_dec, l=l, **mix_params)
        xs, *ffn_w = _ffn(xs, mod_dec, g_pre_ffn, g_post_ffn, (w_gu, w_down), l,
                          tm=md, tf=512, per_row=True, tiles_per_seq=1)
        cs_l.append(ext[:, t_new:])
        q, k, v, u = _inproj(xp, mod, g_pre_mix, w_in_b, l, tm=tm_i, per_row=False, tiles_per_seq=seq // tm_i)
        o, wkp, wvp = _attn_prompt(slopes, q.reshape(bp, seq, ATT_W), k.reshape(bp, seq, ATT_W),
                                   v.reshape(bp, seq, ATT_W), wkp, wvp, l, depth)
        xp = _mixout_prompt(xp, o.reshape(bp * seq, ATT_W), u, mod, l=l, tm=tm_p, tiles_per_seq=tps, **mix_params)
        xp = _ffn(xp, mod, g_pre_ffn, g_post_ffn, tuple(ffn_w), l,
                  tm=tm_f, tf=512, per_row=False, tiles_per_seq=seq // tm_f)
        cp_l.append(u.reshape(bp, seq, CONV_CH)[:, seq - (CONV_K - 1):])
    return (xp.reshape(bp, seq, D_MODEL), xs.reshape(bd, t_new, D_MODEL),
            _position_major(wkp), _position_major(wvp), jnp.stack(cp_l),
            _position_major(wks), _position_major(wvs), jnp.stack(cs_l))
```

```python
import functools
import math

import numpy as np
import jax
import jax.numpy as jnp
from jax import lax
from jax.experimental import pallas as pl
from jax.experimental.pallas import tpu as pltpu

D_MODEL = 2048
ATT_W = 1024
HEAD_DIM = 64
N_HEADS = 16
CONV_CH = 1024
CONV_K = 31
D_FF = 5632
N_MOD = 6
EPS = 1e-6
ATTN_SCALE = HEAD_DIM ** -0.5
LOG2E = math.log2(math.e)
QK_SCALE = ATTN_SCALE * LOG2E
NEG = -1e30
BRANCHES = ((128, 1), (512, 4), (2048, 16))
STEPS = 128
LANES = 128
SUBLANES = 8
HALO = 32
MOD_ROWS = 16
NORM_ROWS = 16
VMEM_LIMIT = 56 * 1024 * 1024

F32 = jnp.float32
BF16 = jnp.bfloat16
NT = (((1,), (1,)), ((), ()))


def _alibi_slopes(n):
    def pow2(m):
        start = 2.0 ** (-8.0 / m)
        return [start ** (i + 1) for i in range(m)]
    if math.log2(n).is_integer():
        s = pow2(n)
    else:
        m = 2 ** math.floor(math.log2(n))
        s = pow2(m) + pow2(2 * m)[0::2][: n - m]
    return np.array(s, dtype=np.float32)


def _rms(x, g):
    return x * lax.rsqrt(jnp.mean(x * x, axis=-1, keepdims=True) + EPS) * g


def _silu(x):
    return x * jax.nn.sigmoid(x)


def _params(sem):
    return pltpu.CompilerParams(dimension_semantics=sem, vmem_limit_bytes=VMEM_LIMIT)


def _layer_resident(shape, l):
    nd = len(shape) - 1
    return pl.BlockSpec((None,) + tuple(shape[1:]), lambda *_: (l,) + (0,) * nd, pipeline_mode=pl.Buffered(1))


def _layer_vec(w, l):
    return pl.BlockSpec((None, 1, w), lambda *_: (l, 0, 0))


def _mod_kernel(c_ref, w_ref, b_ref, o_ref):
    s = _silu(c_ref[...]).astype(BF16)
    o_ref[...] = jnp.dot(s, w_ref[...].astype(BF16), preferred_element_type=F32) + b_ref[...]


def _modulation(c_all, w_mod, b_mod):
    depth = w_mod.shape[0]
    tn = 1024
    nj = D_MODEL // tn
    return pl.pallas_call(
        _mod_kernel,
        grid=(depth, N_MOD, nj),
        in_specs=[
            pl.BlockSpec((MOD_ROWS, D_MODEL), lambda l, m, j: (0, 0)),
            pl.BlockSpec((None, D_MODEL, tn), lambda l, m, j: (l, 0, m * nj + j)),
            pl.BlockSpec((None, 1, tn), lambda l, m, j: (l, 0, m * nj + j)),
        ],
        out_specs=pl.BlockSpec((None, None, MOD_ROWS, tn), lambda l, m, j: (l, m, 0, j)),
        out_shape=jax.ShapeDtypeStruct((depth, N_MOD, MOD_ROWS, D_MODEL), F32),
        compiler_params=_params(("arbitrary", "arbitrary", "arbitrary")),
        name="modulation",
    )(c_all, w_mod, b_mod.reshape(depth, 1, N_MOD * D_MODEL))


def _mod_rows(ref, per_row, tiles_per_seq):
    if per_row:
        return ref[...]
    return ref[pl.ds(pl.program_id(0) // tiles_per_seq, 1), :]


def _mod_spec(per_row, tm, l, m):
    if per_row:
        return pl.BlockSpec((None, None, tm, D_MODEL), lambda i, *_: (l, m, i, 0))
    return pl.BlockSpec((None, None, MOD_ROWS, D_MODEL), lambda i, *_: (l, m, 0, 0))


def _inproj_kernel(x_ref, sc_ref, sh_ref, g_ref, w_ref, q_ref, k_ref, v_ref, u_ref, *, per_row, tiles_per_seq):
    sc = _mod_rows(sc_ref, per_row, tiles_per_seq)
    sh = _mod_rows(sh_ref, per_row, tiles_per_seq)
    h = (_rms(x_ref[...], g_ref[...]) * (1.0 + sc) + sh).astype(BF16)

    def proj(c):
        return jnp.dot(h, w_ref[:, c * ATT_W:(c + 1) * ATT_W], preferred_element_type=F32)

    q_ref[...] = proj(0)
    k_ref[...] = proj(1)
    v_ref[...] = proj(2)
    u_ref[...] = proj(3) * jax.nn.sigmoid(proj(4))


def _inproj(x, mod, g_pre, w_in, l, *, tm, per_row, tiles_per_seq):
    m_rows = x.shape[0]
    row = lambda w: pl.BlockSpec((tm, w), lambda i: (i, 0))
    out = jax.ShapeDtypeStruct((m_rows, ATT_W), F32)
    return pl.pallas_call(
        functools.partial(_inproj_kernel, per_row=per_row, tiles_per_seq=tiles_per_seq),
        grid=(m_rows // tm,),
        in_specs=[
            row(D_MODEL),
            _mod_spec(per_row, tm, l, 1),
            _mod_spec(per_row, tm, l, 0),
            _layer_vec(D_MODEL, l),
            _layer_resident(w_in.shape, l),
        ],
        out_specs=[row(ATT_W)] * 4,
        out_shape=[out] * 4,
        compiler_params=_params(("arbitrary",)),
        name="inproj",
    )(x, mod, mod, g_pre, w_in)


def _attn_prompt_kernel(slopes_ref, q_ref, k_ref, v_ref, *rest, seq, chained):
    o_ref, kt_ref, vt_ref, x4_scr, res_scr, bias_scr = rest[2:] if chained else rest
    hp = pl.program_id(1)
    lane = lax.broadcasted_iota(jnp.int32, (1, LANES), 1)
    head0 = lane < HEAD_DIM
    plane = seq // 4

    qi = lax.broadcasted_iota(jnp.int32, (STEPS, 2 * STEPS), 0)
    ki = lax.broadcasted_iota(jnp.int32, (STEPS, 2 * STEPS), 1)
    j = STEPS + qi - ki
    valid = (j >= 0) & (j <= STEPS)
    jf = j.astype(F32)
    for h in range(2):
        slope = slopes_ref[2 * hp + h]
        for bi, (_, d) in enumerate(BRANCHES):
            bias_scr[bi, h * STEPS:(h + 1) * STEPS, :] = jnp.where(valid, -(slope * d * LOG2E) * jf, NEG)

    for c, ref in enumerate((q_ref, k_ref, v_ref)):
        for r in range(4):
            x = ref[pl.ds(r, plane, stride=4), :]
            x4_scr[c, r * plane:(r + 1) * plane, :] = x * QK_SCALE if c == 0 else x
    for c in range(seq // LANES):
        cols = slice(c * LANES, (c + 1) * LANES)
        kt_ref[:, cols] = k_ref[cols, :].T
        vt_ref[:, cols] = v_ref[cols, :].T

    def attend(q, k_parts, v_parts, bi):
        nk = STEPS * len(k_parts)
        kc = (jnp.concatenate(k_parts, axis=0) if len(k_parts) > 1 else k_parts[0]).astype(BF16)
        vc = (jnp.concatenate(v_parts, axis=0) if len(v_parts) > 1 else v_parts[0]).astype(BF16)
        vc1 = jnp.concatenate([vc, jnp.ones((nk, LANES), BF16)], axis=1)
        q2 = jnp.concatenate([jnp.where(head0, q, 0.0), jnp.where(head0, 0.0, q)], axis=0).astype(BF16)
        s = lax.dot_general(q2, kc, NT, preferred_element_type=F32)
        s = s + (bias_scr[bi] if nk == 2 * STEPS else bias_scr[bi, :, STEPS:])
        m = jnp.max(s, axis=-1, keepdims=True)
        p = jnp.exp2(s - m).astype(BF16)
        r = jnp.dot(p, vc1, preferred_element_type=F32)
        num = jnp.where(head0, r[:STEPS, :LANES], r[STEPS:, :LANES])
        den = jnp.where(head0, r[:STEPS, LANES:], r[STEPS:, LANES:])
        mx = jnp.where(head0, m[:STEPS], m[STEPS:])
        return num * (1.0 / den), mx + jnp.log2(den)

    def block1(start, first):
        own = pl.ds(start, STEPS)
        q = q_ref[own, :] * QK_SCALE
        if first:
            o, lse = attend(q, [k_ref[own, :]], [v_ref[own, :]], 0)
        else:
            prev = pl.ds(start - STEPS, STEPS)
            o, lse = attend(q, [k_ref[prev, :], k_ref[own, :]], [v_ref[prev, :], v_ref[own, :]], 0)
        res_scr[0, own, :] = o
        res_scr[1, own, :] = lse

    for n in range(seq // STEPS):
        block1(n * STEPS, n == 0)

    def block4(base, first):
        own = pl.ds(base, STEPS)
        q = x4_scr[0, own, :]
        if first:
            o, lse = attend(q, [x4_scr[1, own, :]], [x4_scr[2, own, :]], 1)
        else:
            prev = pl.ds(base - STEPS, STEPS)
            o, lse = attend(q, [x4_scr[1, prev, :], x4_scr[1, own, :]], [x4_scr[2, prev, :], x4_scr[2, own, :]], 1)
        res_scr[2, own, :] = o
        res_scr[3, own, :] = lse

    for n in range(plane // STEPS):
        for r in range(4):
            block4(r * plane + n * STEPS, n == 0)

    for r4 in range(4):
        for rp in range(4):
            rows = pl.ds(r4 * plane + rp, STEPS, stride=4)
            o, lse = attend(x4_scr[0, rows, :], [x4_scr[1, rows, :]], [x4_scr[2, rows, :]], 2)
            res_scr[4, rows, :] = o
            res_scr[5, rows, :] = lse

    def merge(r4, carry):
        for c in range(plane // STEPS):
            mine = pl.ds(pl.multiple_of(r4 * plane + c * STEPS, STEPS), STEPS)
            nat = pl.ds(r4 + 4 * c * STEPS, STEPS, stride=4)
            l0, l1, l2 = res_scr[1, nat, :], res_scr[3, mine, :], res_scr[5, mine, :]
            m = jnp.maximum(jnp.maximum(l0, l1), l2)
            e0, e1, e2 = jnp.exp2(l0 - m), jnp.exp2(l1 - m), jnp.exp2(l2 - m)
            num = e0 * res_scr[0, nat, :] + e1 * res_scr[2, mine, :] + e2 * res_scr[4, mine, :]
            o_ref[nat, :] = num * (1.0 / (e0 + e1 + e2))
        return carry
    lax.fori_loop(0, 4, merge, 0)


def _attn_prompt(slopes, q, k, v, win_k, win_v, l, depth):
    b, seq, _ = q.shape
    chained = win_k is not None
    blk = pl.BlockSpec((None, seq, LANES), lambda i, j: (i, 0, j))
    win = pl.BlockSpec((None, None, LANES, seq), lambda i, j: (l, i, j, 0))
    win_shape = jax.ShapeDtypeStruct((depth, b, ATT_W, seq), F32)
    any_spec = pl.BlockSpec(memory_space=pl.ANY)
    return pl.pallas_call(
        functools.partial(_attn_prompt_kernel, seq=seq, chained=chained),
        grid=(b, ATT_W // LANES),
        in_specs=[pl.BlockSpec(memory_space=pltpu.SMEM), blk, blk, blk] + ([any_spec, any_spec] if chained else []),
        out_specs=[blk, win, win],
        out_shape=[jax.ShapeDtypeStruct(q.shape, F32), win_shape, win_shape],
        input_output_aliases={4: 1, 5: 2} if chained else {},
        scratch_shapes=[
            pltpu.VMEM((3, seq, LANES), F32),
            pltpu.VMEM((6, seq, LANES), F32),
            pltpu.VMEM((3, 2 * STEPS, 2 * STEPS), F32),
        ],
        compiler_params=_params(("arbitrary", "arbitrary")),
        name="attn_prompt",
    )(slopes, q, k, v, *((win_k, win_v) if chained else ()))


def _branch_count(delta):
    cnt = jnp.zeros(delta.shape, F32)
    for w, d in BRANCHES:
        ok = (delta >= 0) & (delta <= w) & ((delta & (d - 1)) == 0)
        cnt = cnt + ok.astype(F32)
    return cnt


def _attn_decode_kernel(slopes_ref, q_ref, kn_ref, vn_ref, kc_ref, vc_ref, *rest, n_past, t_new, chained):
    o_ref, ko_ref, vo_ref = rest[2:] if chained else rest
    hp = pl.program_id(1)
    lane = lax.broadcasted_iota(jnp.int32, (1, LANES), 1)
    head0 = lane < HEAD_DIM
    keep = n_past - t_new

    def shifted(cache_ref, new_ref, out_ref):
        rolled = pltpu.roll(cache_ref[...], n_past - t_new, axis=1)
        new_t = jnp.concatenate([new_ref[...], jnp.zeros((LANES - t_new, LANES), F32)], axis=0).T
        new_t = pltpu.roll(new_t, LANES - t_new, axis=1)
        out_ref[:, :n_past - LANES] = rolled[:, :n_past - LANES]
        out_ref[:, n_past - LANES:] = jnp.where(lane >= LANES - t_new, new_t, rolled[:, n_past - LANES:])
    shifted(kc_ref, kn_ref, ko_ref)
    shifted(vc_ref, vn_ref, vo_ref)

    rows = 2 * t_new
    ti = lax.broadcasted_iota(jnp.int32, (rows, n_past), 0) % t_new
    ci = lax.broadcasted_iota(jnp.int32, (rows, n_past), 1)
    delta_c = n_past + ti - ci
    cnt_c = _branch_count(delta_c)
    delta_w = ti - (ci - keep)
    cnt_w = jnp.where(ci >= keep, _branch_count(delta_w), 0.0)
    row = lax.broadcasted_iota(jnp.int32, (rows, 1), 0)
    slope = jnp.where(row < t_new, slopes_ref[2 * hp], slopes_ref[2 * hp + 1])

    q = q_ref[...] * ATTN_SCALE
    q2 = jnp.concatenate([jnp.where(head0, q, 0.0), jnp.where(head0, 0.0, q)], axis=0).astype(BF16)
    s_c = jnp.dot(q2, kc_ref[...].astype(BF16), preferred_element_type=F32)
    s_w = jnp.dot(q2, ko_ref[...].astype(BF16), preferred_element_type=F32)
    s_c = jnp.where(cnt_c > 0, s_c - slope * delta_c.astype(F32), NEG)
    s_w = jnp.where(cnt_w > 0, s_w - slope * delta_w.astype(F32), NEG)
    m = jnp.maximum(jnp.max(s_c, axis=-1, keepdims=True), jnp.max(s_w, axis=-1, keepdims=True))
    p_c = cnt_c * jnp.exp(s_c - m)
    p_w = cnt_w * jnp.exp(s_w - m)
    den = jnp.sum(p_c, axis=-1, keepdims=True) + jnp.sum(p_w, axis=-1, keepdims=True)
    num = (lax.dot_general(p_c.astype(BF16), vc_ref[...].astype(BF16), NT, preferred_element_type=F32)
           + lax.dot_general(p_w.astype(BF16), vo_ref[...].astype(BF16), NT, preferred_element_type=F32))
    o = num * (1.0 / den)
    o_ref[...] = jnp.where(head0, o[:t_new], o[t_new:])


def _attn_decode(slopes, q, k, v, cache_k, cache_v, win_k, win_v, l):
    b, t_new, _ = q.shape
    depth, _, _, n_past = cache_k.shape
    chained = win_k is not None
    new = pl.BlockSpec((None, t_new, LANES), lambda i, j: (i, 0, j))
    win = pl.BlockSpec((None, None, LANES, n_past), lambda i, j: (l, i, j, 0))
    win_shape = jax.ShapeDtypeStruct(cache_k.shape, F32)
    any_spec = pl.BlockSpec(memory_space=pl.ANY)
    return pl.pallas_call(
        functools.partial(_attn_decode_kernel, n_past=n_past, t_new=t_new, chained=chained),
        grid=(b, ATT_W // LANES),
        in_specs=[pl.BlockSpec(memory_space=pltpu.SMEM), new, new, new, win, win]
        + ([any_spec, any_spec] if chained else []),
        out_specs=[new, win, win],
        out_shape=[jax.ShapeDtypeStruct(q.shape, F32), win_shape, win_shape],
        input_output_aliases={6: 1, 7: 2} if chained else {},
        compiler_params=_params(("arbitrary", "arbitrary")),
        name="attn_decode",
    )(slopes, q, k, v, cache_k, cache_v, *((win_k, win_v) if chained else ()))


def _conv_act(y, cb_ref, lng_ref, lnb_ref):
    y = y + cb_ref[...]
    mu = jnp.mean(y, axis=-1, keepdims=True)
    yc = y - mu
    yn = yc * lax.rsqrt(jnp.mean(yc * yc, axis=-1, keepdims=True) + EPS) * lng_ref[...] + lnb_ref[...]
    return _silu(yn).astype(BF16)


def _mix_out(x, o, ya, gt, gao_ref, wout_ref, gpost_ref):
    on = _rms(o, gao_ref[...]).astype(BF16)
    mix = (jnp.dot(on, wout_ref[0:ATT_W, :], preferred_element_type=F32)
           + jnp.dot(ya, wout_ref[ATT_W:, :], preferred_element_type=F32))
    return x + gt * _rms(mix, gpost_ref[...])


def _mixout_prompt_kernel(x_ref, o_ref, u_ref, halo_ref, gt_ref, gao_ref, cw_ref, cb_ref, lng_ref, lnb_ref,
                          wout_ref, gpost_ref, out_ref, ext_scr, y_scr, *, tm, tiles_per_seq):
    first = pl.program_id(0) % tiles_per_seq == 0
    n_ct = CONV_CH // LANES
    for c in range(n_ct):
        lanes = slice(c * LANES, (c + 1) * LANES)
        ext_scr[c, 0:HALO, :] = jnp.where(first, 0.0, halo_ref[:, lanes])
        ext_scr[c, HALO:HALO + tm, :] = u_ref[:, lanes]
    off = HALO - (CONV_K - 1)
    half = tm // 2
    for c in range(n_ct):
        lanes = slice(c * LANES, (c + 1) * LANES)
        acc = [None, None]
        for k in range(CONV_K):
            w = jnp.broadcast_to(cw_ref[k:k + 1, lanes], (half, LANES))
            for par in range(2):
                term = w * ext_scr[c, pl.ds(par + off + k, half, stride=2), :]
                acc[par] = term if acc[par] is None else acc[par] + term
        for par in range(2):
            y_scr[c, pl.ds(par, half, stride=2), :] = acc[par]
    y = jnp.concatenate([y_scr[c] for c in range(n_ct)], axis=1)
    ya = _conv_act(y, cb_ref, lng_ref, lnb_ref)
    gt = _mod_rows(gt_ref, False, tiles_per_seq)
    out_ref[...] = _mix_out(x_ref[...], o_ref[...], ya, gt, gao_ref, wout_ref, gpost_ref)


def _mixout_decode_kernel(x_ref, o_ref, ext_ref, gt_ref, gao_ref, cw_ref, cb_ref, lng_ref, lnb_ref,
                          wout_ref, gpost_ref, out_ref, y_scr, *, n_seq, t_new):
    cw = 256
    for b in range(n_seq):
        for c0 in range(0, CONV_CH, cw):
            acc = None
            for k in range(CONV_K):
                term = cw_ref[k:k + 1, c0:c0 + cw] * ext_ref[b, k:k + t_new, c0:c0 + cw]
                acc = term if acc is None else acc + term
            y_scr[b * t_new:(b + 1) * t_new, c0:c0 + cw] = acc
    ya = _conv_act(y_scr[...], cb_ref, lng_ref, lnb_ref)
    out_ref[...] = _mix_out(x_ref[...], o_ref[...], ya, gt_ref[...], gao_ref, wout_ref, gpost_ref)


def _mixout_prompt(x, o, u, mod, g_attn_out, conv_w, conv_b, ln_g, ln_b, w_out, g_post, l, *, tm, tiles_per_seq):
    m_rows = x.shape[0]
    row = lambda w: pl.BlockSpec((tm, w), lambda i: (i, 0))
    halo = pl.BlockSpec((HALO, CONV_CH), lambda i: (jnp.maximum(i * (tm // HALO) - 1, 0), 0))
    return pl.pallas_call(
        functools.partial(_mixout_prompt_kernel, tm=tm, tiles_per_seq=tiles_per_seq),
        grid=(m_rows // tm,),
        in_specs=[row(D_MODEL), row(ATT_W), row(CONV_CH), halo, _mod_spec(False, tm, l, 2), _layer_vec(ATT_W, l),
                  pl.BlockSpec((None, CONV_K, CONV_CH), lambda i: (l, 0, 0)),
                  _layer_vec(CONV_CH, l), _layer_vec(CONV_CH, l), _layer_vec(CONV_CH, l),
                  _layer_resident(w_out.shape, l), _layer_vec(D_MODEL, l)],
        out_specs=row(D_MODEL),
        out_shape=jax.ShapeDtypeStruct(x.shape, F32),
        scratch_shapes=[pltpu.VMEM((CONV_CH // LANES, HALO + tm, LANES), F32),
                        pltpu.VMEM((CONV_CH // LANES, tm, LANES), F32)],
        compiler_params=_params(("arbitrary",)),
        name="mixout_prompt",
    )(x, o, u, u, mod, g_attn_out, conv_w, conv_b, ln_g, ln_b, w_out, g_post)


def _mixout_decode(x, o, ext, mod_rows, g_attn_out, conv_w, conv_b, ln_g, ln_b, w_out, g_post, l):
    m_rows = x.shape[0]
    n_seq, ext_len, _ = ext.shape
    t_new = ext_len - (CONV_K - 1)
    row = lambda w: pl.BlockSpec((m_rows, w), lambda i: (0, 0))
    return pl.pallas_call(
        functools.partial(_mixout_decode_kernel, n_seq=n_seq, t_new=t_new),
        grid=(1,),
        in_specs=[row(D_MODEL), row(ATT_W), pl.BlockSpec(ext.shape, lambda i: (0, 0, 0)),
                  _mod_spec(True, m_rows, l, 2), _layer_vec(ATT_W, l),
                  pl.BlockSpec((None, CONV_K, CONV_CH), lambda i: (l, 0, 0)),
                  _layer_vec(CONV_CH, l), _layer_vec(CONV_CH, l), _layer_vec(CONV_CH, l),
                  _layer_resident(w_out.shape, l), _layer_vec(D_MODEL, l)],
        out_specs=row(D_MODEL),
        out_shape=jax.ShapeDtypeStruct(x.shape, F32),
        scratch_shapes=[pltpu.VMEM((m_rows, CONV_CH), F32)],
        compiler_params=_params(("arbitrary",)),
        name="mixout_decode",
    )(x, o, ext, mod_rows, g_attn_out, conv_w, conv_b, ln_g, ln_b, w_out, g_post)


def _ffn_kernel(x_ref, sc_ref, sh_ref, gt_ref, gpre_ref, gpost_ref, wg_ref, wu_ref, wd_ref, out_ref, *rest,
                per_row, tiles_per_seq, emit_w):
    h_scr, acc_scr = rest[-2:]
    wg, wu, wd = wg_ref[...], wu_ref[...], wd_ref[...]
    if emit_w:
        wg, wu, wd = wg.astype(BF16), wu.astype(BF16), wd.astype(BF16)
        rest[0][...] = wg
        rest[1][...] = wu
        rest[2][...] = wd
    f = pl.program_id(1)
    tm = x_ref.shape[0]

    def by_chunks(fn):
        for c in range(tm // NORM_ROWS):
            rows = slice(c * NORM_ROWS, (c + 1) * NORM_ROWS)
            fn(rows, lambda ref: ref[rows, :] if per_row else _mod_rows(ref, False, tiles_per_seq))

    @pl.when(f == 0)
    def _():
        def chunk(rows, mod):
            x = x_ref[rows, :]
            inv = lax.rsqrt(jnp.mean(x * x, axis=-1, keepdims=True) + EPS)
            h_scr[rows, :] = (x * inv * (gpre_ref[...] * (1.0 + mod(sc_ref))) + mod(sh_ref)).astype(BF16)
            acc_scr[rows, :] = jnp.zeros((NORM_ROWS, D_MODEL), F32)
        by_chunks(chunk)

    h = h_scr[...]
    gate = jnp.dot(h, wg, preferred_element_type=F32)
    up = jnp.dot(h, wu, preferred_element_type=F32)
    act = (_silu(gate) * up).astype(BF16)
    acc_scr[...] += jnp.dot(act, wd, preferred_element_type=F32)

    @pl.when(f == pl.num_programs(1) - 1)
    def _():
        def chunk(rows, mod):
            acc = acc_scr[rows, :]
            inv = lax.rsqrt(jnp.mean(acc * acc, axis=-1, keepdims=True) + EPS)
            out_ref[rows, :] = x_ref[rows, :] + acc * inv * (gpost_ref[...] * mod(gt_ref))
        by_chunks(chunk)


def _ffn(x, mod, g_pre, g_post, weights, l, *, tm, tf, per_row, tiles_per_seq):
    m_rows = x.shape[0]
    nf = D_FF // tf
    emit_w = len(weights) == 2
    row = pl.BlockSpec((tm, D_MODEL), lambda i, f: (i, 0))
    col = pl.BlockSpec((D_MODEL, tf), lambda i, f: (0, f))
    chunk = pl.BlockSpec((tf, D_MODEL), lambda i, f: (f, 0))
    if emit_w:
        assert m_rows == tm
        w_gu, w_down = weights
        w_args = (w_gu, w_gu, w_down)
        w_specs = [pl.BlockSpec((None, D_MODEL, tf), lambda i, f: (l, 0, f)),
                   pl.BlockSpec((None, D_MODEL, tf), lambda i, f: (l, 0, f + nf)),
                   pl.BlockSpec((None, tf, D_MODEL), lambda i, f: (l, f, 0))]
        out_specs = [row, col, col, chunk]
        out_shape = [jax.ShapeDtypeStruct(x.shape, F32), jax.ShapeDtypeStruct((D_MODEL, D_FF), BF16),
                     jax.ShapeDtypeStruct((D_MODEL, D_FF), BF16), jax.ShapeDtypeStruct((D_FF, D_MODEL), BF16)]
    else:
        w_args = weights
        w_specs = [col, col, chunk]
        out_specs = row
        out_shape = jax.ShapeDtypeStruct(x.shape, F32)
    return pl.pallas_call(
        functools.partial(_ffn_kernel, per_row=per_row, tiles_per_seq=tiles_per_seq, emit_w=emit_w),
        grid=(m_rows // tm, nf),
        in_specs=[row, _mod_spec(per_row, tm, l, 4), _mod_spec(per_row, tm, l, 3), _mod_spec(per_row, tm, l, 5),
                  _layer_vec(D_MODEL, l), _layer_vec(D_MODEL, l)] + w_specs,
        out_specs=out_specs,
        out_shape=out_shape,
        scratch_shapes=[pltpu.VMEM((tm, D_MODEL), BF16), pltpu.VMEM((tm, D_MODEL), F32)],
        compiler_params=_params(("arbitrary", "arbitrary")),
        name="ffn",
    )(x, mod, mod, mod, g_pre, g_post, *w_args)


def _feature_major(win):
    depth, b, p, h, dh = win.shape
    return jnp.transpose(win, (0, 1, 3, 4, 2)).reshape(depth, b, h * dh, p)


def _position_major(win):
    depth, b, w, p = win.shape
    return jnp.transpose(win.reshape(depth, b, N_HEADS, HEAD_DIM, p), (0, 1, 4, 2, 3))


def kernel(x_prompt, x_sample, c_prompt, c_sample, cache_win_k, cache_win_v, state_conv, g_pre_mix, g_post_mix, g_pre_ffn, g_post_ffn, w_mod, b_mod, w_in, g_attn_out, conv_w, conv_b, conv_ln_g, conv_ln_b, w_out, w_gu, w_down):
    depth = w_mod.shape[0]
    bp, seq, _ = x_prompt.shape
    bd, t_new, _ = x_sample.shape
    n_past = cache_win_k.shape[2]
    assert bp + bd <= MOD_ROWS and t_new == SUBLANES
    assert seq == STEPS * BRANCHES[-1][1] and n_past >= BRANCHES[-1][0] and seq >= BRANCHES[-1][0]

    slopes = jnp.asarray(_alibi_slopes(N_HEADS))
    c_all = jnp.zeros((MOD_ROWS, D_MODEL), F32).at[:bp].set(c_prompt).at[bp:bp + bd].set(c_sample)
    mod = _modulation(c_all, w_mod, b_mod)
    mod_dec = jnp.repeat(mod[:, :, bp:bp + bd], t_new, axis=2)

    w_in_b, w_out_b = w_in.astype(BF16), w_out.astype(BF16)
    vec = lambda p: p.reshape(depth, 1, p.shape[-1])
    g_pre_mix, g_post_mix, g_pre_ffn, g_post_ffn = vec(g_pre_mix), vec(g_post_mix), vec(g_pre_ffn), vec(g_post_ffn)
    mix_params = dict(g_attn_out=vec(g_attn_out), conv_w=conv_w, conv_b=vec(conv_b), ln_g=vec(conv_ln_g),
                      ln_b=vec(conv_ln_b), w_out=w_out_b, g_post=g_post_mix)
    cache_k = _feature_major(cache_win_k)
    cache_v = _feature_major(cache_win_v)

    tm_i = 512
    tm_p = 256
    tps = seq // tm_p
    tm_f = 512
    xp = x_prompt.reshape(bp * seq, D_MODEL)
    xs = x_sample.reshape(bd * t_new, D_MODEL)
    md = bd * t_new
    wkp = wvp = wks = wvs = None
    cp_l, cs_l = [], []
    for l in range(depth):
        q, k, v, u = _inproj(xs, mod_dec, g_pre_mix, w_in_b, l, tm=md, per_row=True, tiles_per_seq=1)
        o, wks, wvs = _attn_decode(slopes, q.reshape(bd, t_new, ATT_W), k.reshape(bd, t_new, ATT_W),
                                   v.reshape(bd, t_new, ATT_W), cache_k, cache_v, wks, wvs, l)
        ext = jnp.concatenate([state_conv[l], u.reshape(bd, t_new, CONV_CH)], axis=1)
        xs = _mixout_decode(xs, o.reshape(md, ATT_W), ext, mod_dec, l=l, **mix_params)
        xs, *ffn_w = _ffn(xs, mod_dec, g_pre_ffn, g_post_ffn, (w_gu, w_down), l,
                          tm=md, tf=512, per_row=True, tiles_per_seq=1)
        cs_l.append(ext[:, t_new:])
        q, k, v, u = _inproj(xp, mod, g_pre_mix, w_in_b, l, tm=tm_i, per_row=False, tiles_per_seq=seq // tm_i)
        o, wkp, wvp = _attn_prompt(slopes, q.reshape(bp, seq, ATT_W), k.reshape(bp, seq, ATT_W),
                                   v.reshape(bp, seq, ATT_W), wkp, wvp, l, depth)
        xp = _mixout_prompt(xp, o.reshape(bp * seq, ATT_W), u, mod, l=l, tm=tm_p, tiles_per_seq=tps, **mix_params)
        xp = _ffn(xp, mod, g_pre_ffn, g_post_ffn, tuple(ffn_w), l,
                  tm=tm_f, tf=512, per_row=False, tiles_per_seq=seq // tm_f)
        cp_l.append(u.reshape(bp, seq, CONV_CH)[:, seq - (CONV_K - 1):])
    return (xp.reshape(bp, seq, D_MODEL), xs.reshape(bd, t_new, D_MODEL),
            _position_major(wkp), _position_major(wvp), jnp.stack(cp_l),
            _position_major(wks), _position_major(wvs), jnp.stack(cs_l))
```

```python
import functools
import math

import numpy as np
import jax
import jax.numpy as jnp
from jax import lax
from jax.experimental import pallas as pl
from jax.experimental.pallas import tpu as pltpu

D_MODEL = 2048
ATT_W = 1024
HEAD_DIM = 64
N_HEADS = 16
CONV_CH = 1024
CONV_K = 31
D_FF = 5632
N_MOD = 6
EPS = 1e-6
ATTN_SCALE = HEAD_DIM ** -0.5
LOG2E = math.log2(math.e)
QK_SCALE = ATTN_SCALE * LOG2E
NEG = -1e30
BRANCHES = ((128, 1), (512, 4), (2048, 16))
STEPS = 128
LANES = 128
SUBLANES = 8
HALO = 32
MOD_ROWS = 16
NORM_ROWS = 16
VMEM_LIMIT = 56 * 1024 * 1024

F32 = jnp.float32
BF16 = jnp.bfloat16
NT = (((1,), (1,)), ((), ()))


def _alibi_slopes(n):
    def pow2(m):
        start = 2.0 ** (-8.0 / m)
        return [start ** (i + 1) for i in range(m)]
    if math.log2(n).is_integer():
        s = pow2(n)
    else:
        m = 2 ** math.floor(math.log2(n))
        s = pow2(m) + pow2(2 * m)[0::2][: n - m]
    return np.array(s, dtype=np.float32)


def _rms(x, g):
    return x * lax.rsqrt(jnp.mean(x * x, axis=-1, keepdims=True) + EPS) * g


def _silu(x):
    return x * jax.nn.sigmoid(x)


def _params(sem):
    return pltpu.CompilerParams(dimension_semantics=sem, vmem_limit_bytes=VMEM_LIMIT)


def _layer_resident(shape, l):
    nd = len(shape) - 1
    return pl.BlockSpec((None,) + tuple(shape[1:]), lambda *_: (l,) + (0,) * nd, pipeline_mode=pl.Buffered(1))


def _layer_vec(w, l):
    return pl.BlockSpec((None, 1, w), lambda *_: (l, 0, 0))


def _mod_kernel(c_ref, w_ref, b_ref, o_ref):
    s = _silu(c_ref[...]).astype(BF16)
    o_ref[...] = jnp.dot(s, w_ref[...].astype(BF16), preferred_element_type=F32) + b_ref[...]


def _modulation(c_all, w_mod, b_mod):
    depth = w_mod.shape[0]
    tn = 1024
    nj = D_MODEL // tn
    return pl.pallas_call(
        _mod_kernel,
        grid=(depth, N_MOD, nj),
        in_specs=[
            pl.BlockSpec((MOD_ROWS, D_MODEL), lambda l, m, j: (0, 0)),
            pl.BlockSpec((None, D_MODEL, tn), lambda l, m, j: (l, 0, m * nj + j)),
            pl.BlockSpec((None, 1, tn), lambda l, m, j: (l, 0, m * nj + j)),
        ],
        out_specs=pl.BlockSpec((None, None, MOD_ROWS, tn), lambda l, m, j: (l, m, 0, j)),
        out_shape=jax.ShapeDtypeStruct((depth, N_MOD, MOD_ROWS, D_MODEL), F32),
        compiler_params=_params(("arbitrary", "arbitrary", "arbitrary")),
        name="modulation",
    )(c_all, w_mod, b_mod.reshape(depth, 1, N_MOD * D_MODEL))


def _mod_rows(ref, per_row, tiles_per_seq):
    if per_row:
        return ref[...]
    return ref[pl.ds(pl.program_id(0) // tiles_per_seq, 1), :]


def _mod_spec(per_row, tm, l, m):
    if per_row:
        return pl.BlockSpec((None, None, tm, D_MODEL), lambda i, *_: (l, m, i, 0))
    return pl.BlockSpec((None, None, MOD_ROWS, D_MODEL), lambda i, *_: (l, m, 0, 0))


def _inproj_kernel(x_ref, sc_ref, sh_ref, g_ref, w_ref, q_ref, k_ref, v_ref, u_ref, *, per_row, tiles_per_seq):
    sc = _mod_rows(sc_ref, per_row, tiles_per_seq)
    sh = _mod_rows(sh_ref, per_row, tiles_per_seq)
    h = (_rms(x_ref[...], g_ref[...]) * (1.0 + sc) + sh).astype(BF16)

    def proj(c):
        return jnp.dot(h, w_ref[:, c * ATT_W:(c + 1) * ATT_W], preferred_element_type=F32)

    q_ref[...] = proj(0)
    k_ref[...] = proj(1)
    v_ref[...] = proj(2)
    u_ref[...] = proj(3) * jax.nn.sigmoid(proj(4))


def _inproj(x, mod, g_pre, w_in, l, *, tm, per_row, tiles_per_seq):
    m_rows = x.shape[0]
    row = lambda w: pl.BlockSpec((tm, w), lambda i: (i, 0))
    out = jax.ShapeDtypeStruct((m_rows, ATT_W), F32)
    return pl.pallas_call(
        functools.partial(_inproj_kernel, per_row=per_row, tiles_per_seq=tiles_per_seq),
        grid=(m_rows // tm,),
        in_specs=[
            row(D_MODEL),
            _mod_spec(per_row, tm, l, 1),
            _mod_spec(per_row, tm, l, 0),
            _layer_vec(D_MODEL, l),
            _layer_resident(w_in.shape, l),
        ],
        out_specs=[row(ATT_W)] * 4,
        out_shape=[out] * 4,
        compiler_params=_params(("arbitrary",)),
        name="inproj",
    )(x, mod, mod, g_pre, w_in)


def _attn_prompt_kernel(slopes_ref, q_ref, k_ref, v_ref, *rest, seq, chained):
    o_ref, kt_ref, vt_ref, x4_scr, res_scr, bias_scr = rest[2:] if chained else rest
    hp = pl.program_id(1)
    lane = lax.broadcasted_iota(jnp.int32, (1, LANES), 1)
    head0 = lane < HEAD_DIM
    plane = seq // 4

    qi = lax.broadcasted_iota(jnp.int32, (STEPS, 2 * STEPS), 0)
    ki = lax.broadcasted_iota(jnp.int32, (STEPS, 2 * STEPS), 1)
    j = STEPS + qi - ki
    valid = (j >= 0) & (j <= STEPS)
    jf = j.astype(F32)
    for h in range(2):
        slope = slopes_ref[2 * hp + h]
        for bi, (_, d) in enumerate(BRANCHES):
            bias_scr[bi, h * STEPS:(h + 1) * STEPS, :] = jnp.where(valid, -(slope * d * LOG2E) * jf, NEG)

    for c, ref in enumerate((q_ref, k_ref, v_ref)):
        for r in range(4):
            x = ref[pl.ds(r, plane, stride=4), :]
            x4_scr[c, r * plane:(r + 1) * plane, :] = x * QK_SCALE if c == 0 else x
    for c in range(seq // LANES):
        cols = slice(c * LANES, (c + 1) * LANES)
        kt_ref[:, cols] = k_ref[cols, :].T
        vt_ref[:, cols] = v_ref[cols, :].T

    def attend(q, k_parts, v_parts, bi):
        nk = STEPS * len(k_parts)
        kc = (jnp.concatenate(k_parts, axis=0) if len(k_parts) > 1 else k_parts[0]).astype(BF16)
        vc = (jnp.concatenate(v_parts, axis=0) if len(v_parts) > 1 else v_parts[0]).astype(BF16)
        vc1 = jnp.concatenate([vc, jnp.ones((nk, LANES), BF16)], axis=1)
        q2 = jnp.concatenate([jnp.where(head0, q, 0.0), jnp.where(head0, 0.0, q)], axis=0).astype(BF16)
        s = lax.dot_general(q2, kc, NT, preferred_element_type=F32)
        s = s + (bias_scr[bi] if nk == 2 * STEPS else bias_scr[bi, :, STEPS:])
        m = jnp.max(s, axis=-1, keepdims=True)
        p = jnp.exp2(s - m).astype(BF16)
        r = jnp.dot(p, vc1, preferred_element_type=F32)
        num = jnp.where(head0, r[:STEPS, :LANES], r[STEPS:, :LANES])
        den = jnp.where(head0, r[:STEPS, LANES:], r[STEPS:, LANES:])
        mx = jnp.where(head0, m[:STEPS], m[STEPS:])
        return num * (1.0 / den), mx + jnp.log2(den)

    def block1(start, first):
        own = pl.ds(start, STEPS)
        q = q_ref[own, :] * QK_SCALE
        if first:
            o, lse = attend(q, [k_ref[own, :]], [v_ref[own, :]], 0)
        else:
            prev = pl.ds(start - STEPS, STEPS)
            o, lse = attend(q, [k_ref[prev, :], k_ref[own, :]], [v_ref[prev, :], v_ref[own, :]], 0)
        res_scr[0, own, :] = o
        res_scr[1, own, :] = lse

    for n in range(seq // STEPS):
        block1(n * STEPS, n == 0)

    def block4(base, first):
        own = pl.ds(base, STEPS)
        q = x4_scr[0, own, :]
        if first:
            o, lse = attend(q, [x4_scr[1, own, :]], [x4_scr[2, own, :]], 1)
        else:
            prev = pl.ds(base - STEPS, STEPS)
            o, lse = attend(q, [x4_scr[1, prev, :], x4_scr[1, own, :]], [x4_scr[2, prev, :], x4_scr[2, own, :]], 1)
        res_scr[2, own, :] = o
        res_scr[3, own, :] = lse

    for n in range(plane // STEPS):
        for r in range(4):
            block4(r * plane + n * STEPS, n == 0)

    for r4 in range(4):
        for rp in range(4):
            rows = pl.ds(r4 * plane + rp, STEPS, stride=4)
            o, lse = attend(x4_scr[0, rows, :], [x4_scr[1, rows, :]], [x4_scr[2, rows, :]], 2)
            res_scr[4, rows, :] = o
            res_scr[5, rows, :] = lse

    def merge(r4, carry):
        for c in range(plane // STEPS):
            mine = pl.ds(pl.multiple_of(r4 * plane + c * STEPS, STEPS), STEPS)
            nat = pl.ds(r4 + 4 * c * STEPS, STEPS, stride=4)
            l0, l1, l2 = res_scr[1, nat, :], res_scr[3, mine, :], res_scr[5, mine, :]
            m = jnp.maximum(jnp.maximum(l0, l1), l2)
            e0, e1, e2 = jnp.exp2(l0 - m), jnp.exp2(l1 - m), jnp.exp2(l2 - m)
            num = e0 * res_scr[0, nat, :] + e1 * res_scr[2, mine, :] + e2 * res_scr[4, mine, :]
            o_ref[nat, :] = num * (1.0 / (e0 + e1 + e2))
        return carry
    lax.fori_loop(0, 4, merge, 0)


def _attn_prompt(slopes, q, k, v, win_k, win_v, l, depth):
    b, seq, _ = q.shape
    chained = win_k is not None
    blk = pl.BlockSpec((None, seq, LANES), lambda i, j: (i, 0, j))
    win = pl.BlockSpec((None, None, LANES, seq), lambda i, j: (l, i, j, 0))
    win_shape = jax.ShapeDtypeStruct((depth, b, ATT_W, seq), F32)
    any_spec = pl.BlockSpec(memory_space=pl.ANY)
    return pl.pallas_call(
        functools.partial(_attn_prompt_kernel, seq=seq, chained=chained),
        grid=(b, ATT_W // LANES),
        in_specs=[pl.BlockSpec(memory_space=pltpu.SMEM), blk, blk, blk] + ([any_spec, any_spec] if chained else []),
        out_specs=[blk, win, win],
        out_shape=[jax.ShapeDtypeStruct(q.shape, F32), win_shape, win_shape],
        input_output_aliases={4: 1, 5: 2} if chained else {},
        scratch_shapes=[
            pltpu.VMEM((3, seq, LANES), F32),
            pltpu.VMEM((6, seq, LANES), F32),
            pltpu.VMEM((3, 2 * STEPS, 2 * STEPS), F32),
        ],
        compiler_params=_params(("arbitrary", "arbitrary")),
        name="attn_prompt",
    )(slopes, q, k, v, *((win_k, win_v) if chained else ()))


def _branch_count(delta):
    cnt = jnp.zeros(delta.shape, F32)
    for w, d in BRANCHES:
        ok = (delta >= 0) & (delta <= w) & ((delta & (d - 1)) == 0)
        cnt = cnt + ok.astype(F32)
    return cnt


def _attn_decode_kernel(slopes_ref, q_ref, kn_ref, vn_ref, kc_ref, vc_ref, *rest, n_past, t_new, chained):
    o_ref, ko_ref, vo_ref = rest[2:] if chained else rest
    hp = pl.program_id(1)
    lane = lax.broadcasted_iota(jnp.int32, (1, LANES), 1)
    head0 = lane < HEAD_DIM
    keep = n_past - t_new

    def shifted(cache_ref, new_ref, out_ref):
        rolled = pltpu.roll(cache_ref[...], n_past - t_new, axis=1)
        new_t = jnp.concatenate([new_ref[...], jnp.zeros((LANES - t_new, LANES), F32)], axis=0).T
        new_t = pltpu.roll(new_t, LANES - t_new, axis=1)
        out_ref[:, :n_past - LANES] = rolled[:, :n_past - LANES]
        out_ref[:, n_past - LANES:] = jnp.where(lane >= LANES - t_new, new_t, rolled[:, n_past - LANES:])
    shifted(kc_ref, kn_ref, ko_ref)
    shifted(vc_ref, vn_ref, vo_ref)

    rows = 2 * t_new
    ti = lax.broadcasted_iota(jnp.int32, (rows, n_past), 0) % t_new
    ci = lax.broadcasted_iota(jnp.int32, (rows, n_past), 1)
    delta_c = n_past + ti - ci
    cnt_c = _branch_count(delta_c)
    delta_w = ti - (ci - keep)
    cnt_w = jnp.where(ci >= keep, _branch_count(delta_w), 0.0)
    row = lax.broadcasted_iota(jnp.int32, (rows, 1), 0)
    slope = jnp.where(row < t_new, slopes_ref[2 * hp], slopes_ref[2 * hp + 1])

    q = q_ref[...] * ATTN_SCALE
    q2 = jnp.concatenate([jnp.where(head0, q, 0.0), jnp.where(head0, 0.0, q)], axis=0).astype(BF16)
    s_c = jnp.dot(q2, kc_ref[...].astype(BF16), preferred_element_type=F32)
    s_w = jnp.dot(q2, ko_ref[...].astype(BF16), preferred_element_type=F32)
    s_c = jnp.where(cnt_c > 0, s_c - slope * delta_c.astype(F32), NEG)
    s_w = jnp.where(cnt_w > 0, s_w - slope * delta_w.astype(F32), NEG)
    m = jnp.maximum(jnp.max(s_c, axis=-1, keepdims=True), jnp.max(s_w, axis=-1, keepdims=True))
    p_c = cnt_c * jnp.exp(s_c - m)
    p_w = cnt_w * jnp.exp(s_w - m)
    den = jnp.sum(p_c, axis=-1, keepdims=True) + jnp.sum(p_w, axis=-1, keepdims=True)
    num = (lax.dot_general(p_c.astype(BF16), vc_ref[...].astype(BF16), NT, preferred_element_type=F32)
           + lax.dot_general(p_w.astype(BF16), vo_ref[...].astype(BF16), NT, preferred_element_type=F32))
    o = num * (1.0 / den)
    o_ref[...] = jnp.where(head0, o[:t_new], o[t_new:])


def _attn_decode(slopes, q, k, v, cache_k, cache_v, win_k, win_v, l):
    b, t_new, _ = q.shape
    depth, _, _, n_past = cache_k.shape
    chained = win_k is not None
    new = pl.BlockSpec((None, t_new, LANES), lambda i, j: (i, 0, j))
    win = pl.BlockSpec((None, None, LANES, n_past), lambda i, j: (l, i, j, 0))
    win_shape = jax.ShapeDtypeStruct(cache_k.shape, F32)
    any_spec = pl.BlockSpec(memory_space=pl.ANY)
    return pl.pallas_call(
        functools.partial(_attn_decode_kernel, n_past=n_past, t_new=t_new, chained=chained),
        grid=(b, ATT_W // LANES),
        in_specs=[pl.BlockSpec(memory_space=pltpu.SMEM), new, new, new, win, win]
        + ([any_spec, any_spec] if chained else []),
        out_specs=[new, win, win],
        out_shape=[jax.ShapeDtypeStruct(q.shape, F32), win_shape, win_shape],
        input_output_aliases={6: 1, 7: 2} if chained else {},
        compiler_params=_params(("arbitrary", "arbitrary")),
        name="attn_decode",
    )(slopes, q, k, v, cache_k, cache_v, *((win_k, win_v) if chained else ()))


def _conv_act(y, cb_ref, lng_ref, lnb_ref):
    y = y + cb_ref[...]
    mu = jnp.mean(y, axis=-1, keepdims=True)
    yc = y - mu
    yn = yc * lax.rsqrt(jnp.mean(yc * yc, axis=-1, keepdims=True) + EPS) * lng_ref[...] + lnb_ref[...]
    return _silu(yn).astype(BF16)


def _mix_out(x, o, ya, gt, gao_ref, wout_ref, gpost_ref):
    on = _rms(o, gao_ref[...]).astype(BF16)
    mix = (jnp.dot(on, wout_ref[0:ATT_W, :], preferred_element_type=F32)
           + jnp.dot(ya, wout_ref[ATT_W:, :], preferred_element_type=F32))
    return x + gt * _rms(mix, gpost_ref[...])


def _mixout_prompt_kernel(x_ref, o_ref, u_ref, halo_ref, gt_ref, gao_ref, cw_ref, cb_ref, lng_ref, lnb_ref,
                          wout_ref, gpost_ref, out_ref, ext_scr, y_scr, *, tm, tiles_per_seq):
    first = pl.program_id(0) % tiles_per_seq == 0
    n_ct = CONV_CH // LANES
    for c in range(n_ct):
        lanes = slice(c * LANES, (c + 1) * LANES)
        ext_scr[c, 0:HALO, :] = jnp.where(first, 0.0, halo_ref[:, lanes])
        ext_scr[c, HALO:HALO + tm, :] = u_ref[:, lanes]
    off = HALO - (CONV_K - 1)
    half = tm // 2
    for c in range(n_ct):
        lanes = slice(c * LANES, (c + 1) * LANES)
        acc = [None, None]
        for k in range(CONV_K):
            w = jnp.broadcast_to(cw_ref[k:k + 1, lanes], (half, LANES))
            for par in range(2):
                term = w * ext_scr[c, pl.ds(par + off + k, half, stride=2), :]
                acc[par] = term if acc[par] is None else acc[par] + term
        for par in range(2):
            y_scr[c, pl.ds(par, half, stride=2), :] = acc[par]
    y = jnp.concatenate([y_scr[c] for c in range(n_ct)], axis=1)
    ya = _conv_act(y, cb_ref, lng_ref, lnb_ref)
    gt = _mod_rows(gt_ref, False, tiles_per_seq)
    out_ref[...] = _mix_out(x_ref[...], o_ref[...], ya, gt, gao_ref, wout_ref, gpost_ref)


def _mixout_decode_kernel(x_ref, o_ref, ext_ref, gt_ref, gao_ref, cw_ref, cb_ref, lng_ref, lnb_ref,
                          wout_ref, gpost_ref, out_ref, y_scr, *, n_seq, t_new):
    cw = 256
    for b in range(n_seq):
        for c0 in range(0, CONV_CH, cw):
            acc = None
            for k in range(CONV_K):
                term = cw_ref[k:k + 1, c0:c0 + cw] * ext_ref[b, k:k + t_new, c0:c0 + cw]
                acc = term if acc is None else acc + term
            y_scr[b * t_new:(b + 1) * t_new, c0:c0 + cw] = acc
    ya = _conv_act(y_scr[...], cb_ref, lng_ref, lnb_ref)
    out_ref[...] = _mix_out(x_ref[...], o_ref[...], ya, gt_ref[...], gao_ref, wout_ref, gpost_ref)


def _mixout_prompt(x, o, u, mod, g_attn_out, conv_w, conv_b, ln_g, ln_b, w_out, g_post, l, *, tm, tiles_per_seq):
    m_rows = x.shape[0]
    row = lambda w: pl.BlockSpec((tm, w), lambda i: (i, 0))
    halo = pl.BlockSpec((HALO, CONV_CH), lambda i: (jnp.maximum(i * (tm // HALO) - 1, 0), 0))
    return pl.pallas_call(
        functools.partial(_mixout_prompt_kernel, tm=tm, tiles_per_seq=tiles_per_seq),
        grid=(m_rows // tm,),
        in_specs=[row(D_MODEL), row(ATT_W), row(CONV_CH), halo, _mod_spec(False, tm, l, 2), _layer_vec(ATT_W, l),
                  pl.BlockSpec((None, CONV_K, CONV_CH), lambda i: (l, 0, 0)),
                  _layer_vec(CONV_CH, l), _layer_vec(CONV_CH, l), _layer_vec(CONV_CH, l),
                  _layer_resident(w_out.shape, l), _layer_vec(D_MODEL, l)],
        out_specs=row(D_MODEL),
        out_shape=jax.ShapeDtypeStruct(x.shape, F32),
        scratch_shapes=[pltpu.VMEM((CONV_CH // LANES, HALO + tm, LANES), F32),
                        pltpu.VMEM((CONV_CH // LANES, tm, LANES), F32)],
        compiler_params=_params(("arbitrary",)),
        name="mixout_prompt",
    )(x, o, u, u, mod, g_attn_out, conv_w, conv_b, ln_g, ln_b, w_out, g_post)


def _mixout_decode(x, o, ext, mod_rows, g_attn_out, conv_w, conv_b, ln_g, ln_b, w_out, g_post, l):
    m_rows = x.shape[0]
    n_seq, ext_len, _ = ext.shape
    t_new = ext_len - (CONV_K - 1)
    row = lambda w: pl.BlockSpec((m_rows, w), lambda i: (0, 0))
    return pl.pallas_call(
        functools.partial(_mixout_decode_kernel, n_seq=n_seq, t_new=t_new),
        grid=(1,),
        in_specs=[row(D_MODEL), row(ATT_W), pl.BlockSpec(ext.shape, lambda i: (0, 0, 0)),
                  _mod_spec(True, m_rows, l, 2), _layer_vec(ATT_W, l),
                  pl.BlockSpec((None, CONV_K, CONV_CH), lambda i: (l, 0, 0)),
                  _layer_vec(CONV_CH, l), _layer_vec(CONV_CH, l), _layer_vec(CONV_CH, l),
                  _layer_resident(w_out.shape, l), _layer_vec(D_MODEL, l)],
        out_specs=row(D_MODEL),
        out_shape=jax.ShapeDtypeStruct(x.shape, F32),
        scratch_shapes=[pltpu.VMEM((m_rows, CONV_CH), F32)],
        compiler_params=_params(("arbitrary",)),
        name="mixout_decode",
    )(x, o, ext, mod_rows, g_attn_out, conv_w, conv_b, ln_g, ln_b, w_out, g_post)


def _ffn_kernel(x_ref, sc_ref, sh_ref, gt_ref, gpre_ref, gpost_ref, wg_ref, wu_ref, wd_ref, out_ref, *rest,
                per_row, tiles_per_seq, emit_w):
    h_scr, acc_scr = rest[-2:]
    f = pl.program_id(1)
    tm = x_ref.shape[0]

    def by_chunks(fn):
        for c in range(tm // NORM_ROWS):
            rows = slice(c * NORM_ROWS, (c + 1) * NORM_ROWS)
            fn(rows, lambda ref: ref[rows, :] if per_row else _mod_rows(ref, False, tiles_per_seq))

    @pl.when(f == 0)
    def _():
        def chunk(rows, mod):
            x = x_ref[rows, :]
            inv = lax.rsqrt(jnp.mean(x * x, axis=-1, keepdims=True) + EPS)
            h_scr[rows, :] = (x * inv * (gpre_ref[...] * (1.0 + mod(sc_ref))) + mod(sh_ref)).astype(BF16)
            acc_scr[rows, :] = jnp.zeros((NORM_ROWS, D_MODEL), F32)
        by_chunks(chunk)

    wg, wu, wd = wg_ref[...], wu_ref[...], wd_ref[...]
    if emit_w:
        wg, wu, wd = wg.astype(BF16), wu.astype(BF16), wd.astype(BF16)
        rest[0][...] = wg
        rest[1][...] = wu
        rest[2][...] = wd
    h = h_scr[...]
    gate = jnp.dot(h, wg, preferred_element_type=F32)
    up = jnp.dot(h, wu, preferred_element_type=F32)
    act = (_silu(gate) * up).astype(BF16)
    acc_scr[...] += jnp.dot(act, wd, preferred_element_type=F32)

    @pl.when(f == pl.num_programs(1) - 1)
    def _():
        def chunk(rows, mod):
            acc = acc_scr[rows, :]
            inv = lax.rsqrt(jnp.mean(acc * acc, axis=-1, keepdims=True) + EPS)
            out_ref[rows, :] = x_ref[rows, :] + acc * inv * (gpost_ref[...] * mod(gt_ref))
        by_chunks(chunk)


def _ffn(x, mod, g_pre, g_post, weights, l, *, tm, tf, per_row, tiles_per_seq):
    m_rows = x.shape[0]
    nf = D_FF // tf
    emit_w = len(weights) == 2
    row = pl.BlockSpec((tm, D_MODEL), lambda i, f: (i, 0))
    col = pl.BlockSpec((D_MODEL, tf), lambda i, f: (0, f))
    chunk = pl.BlockSpec((tf, D_MODEL), lambda i, f: (f, 0))
    if emit_w:
        assert m_rows == tm
        w_gu, w_down = weights
        w_args = (w_gu, w_gu, w_down)
        w_specs = [pl.BlockSpec((None, D_MODEL, tf), lambda i, f: (l, 0, f)),
                   pl.BlockSpec((None, D_MODEL, tf), lambda i, f: (l, 0, f + nf)),
                   pl.BlockSpec((None, tf, D_MODEL), lambda i, f: (l, f, 0))]
        out_specs = [row, col, col, chunk]
        out_shape = [jax.ShapeDtypeStruct(x.shape, F32), jax.ShapeDtypeStruct((D_MODEL, D_FF), BF16),
                     jax.ShapeDtypeStruct((D_MODEL, D_FF), BF16), jax.ShapeDtypeStruct((D_FF, D_MODEL), BF16)]
    else:
        w_args = weights
        w_specs = [col, col, chunk]
        out_specs = row
        out_shape = jax.ShapeDtypeStruct(x.shape, F32)
    return pl.pallas_call(
        functools.partial(_ffn_kernel, per_row=per_row, tiles_per_seq=tiles_per_seq, emit_w=emit_w),
        grid=(m_rows // tm, nf),
        in_specs=[row, _mod_spec(per_row, tm, l, 4), _mod_spec(per_row, tm, l, 3), _mod_spec(per_row, tm, l, 5),
                  _layer_vec(D_MODEL, l), _layer_vec(D_MODEL, l)] + w_specs,
        out_specs=out_specs,
        out_shape=out_shape,
        scratch_shapes=[pltpu.VMEM((tm, D_MODEL), BF16), pltpu.VMEM((tm, D_MODEL), F32)],
        compiler_params=_params(("arbitrary", "arbitrary")),
        name="ffn",
    )(x, mod, mod, mod, g_pre, g_post, *w_args)


def _feature_major(win):
    depth, b, p, h, dh = win.shape
    return jnp.transpose(win, (0, 1, 3, 4, 2)).reshape(depth, b, h * dh, p)


def _position_major(win):
    depth, b, w, p = win.shape
    return jnp.transpose(win.reshape(depth, b, N_HEADS, HEAD_DIM, p), (0, 1, 4, 2, 3))


def kernel(x_prompt, x_sample, c_prompt, c_sample, cache_win_k, cache_win_v, state_conv, g_pre_mix, g_post_mix, g_pre_ffn, g_post_ffn, w_mod, b_mod, w_in, g_attn_out, conv_w, conv_b, conv_ln_g, conv_ln_b, w_out, w_gu, w_down):
    depth = w_mod.shape[0]
    bp, seq, _ = x_prompt.shape
    bd, t_new, _ = x_sample.shape
    n_past = cache_win_k.shape[2]
    assert bp + bd <= MOD_ROWS and t_new == SUBLANES
    assert seq == STEPS * BRANCHES[-1][1] and n_past >= BRANCHES[-1][0] and seq >= BRANCHES[-1][0]

    slopes = jnp.asarray(_alibi_slopes(N_HEADS))
    c_all = jnp.zeros((MOD_ROWS, D_MODEL), F32).at[:bp].set(c_prompt).at[bp:bp + bd].set(c_sample)
    mod = _modulation(c_all, w_mod, b_mod)
    mod_dec = jnp.repeat(mod[:, :, bp:bp + bd], t_new, axis=2)

    w_in_b, w_out_b = w_in.astype(BF16), w_out.astype(BF16)
    vec = lambda p: p.reshape(depth, 1, p.shape[-1])
    g_pre_mix, g_post_mix, g_pre_ffn, g_post_ffn = vec(g_pre_mix), vec(g_post_mix), vec(g_pre_ffn), vec(g_post_ffn)
    mix_params = dict(g_attn_out=vec(g_attn_out), conv_w=conv_w, conv_b=vec(conv_b), ln_g=vec(conv_ln_g),
                      ln_b=vec(conv_ln_b), w_out=w_out_b, g_post=g_post_mix)
    cache_k = _feature_major(cache_win_k)
    cache_v = _feature_major(cache_win_v)

    tm_i = 512
    tm_p = 256
    tps = seq // tm_p
    tm_f = 512
    xp = x_prompt.reshape(bp * seq, D_MODEL)
    xs = x_sample.reshape(bd * t_new, D_MODEL)
    md = bd * t_new
    wkp = wvp = wks = wvs = None
    cp_l, cs_l = [], []
    for l in range(depth):
        q, k, v, u = _inproj(xs, mod_dec, g_pre_mix, w_in_b, l, tm=md, per_row=True, tiles_per_seq=1)
        o, wks, wvs = _attn_decode(slopes, q.reshape(bd, t_new, ATT_W), k.reshape(bd, t_new, ATT_W),
                                   v.reshape(bd, t_new, ATT_W), cache_k, cache_v, wks, wvs, l)
        ext = jnp.concatenate([state_conv[l], u.reshape(bd, t_new, CONV_CH)], axis=1)
        xs = _mixout_decode(xs, o.reshape(md, ATT_W), ext, mod_dec, l=l, **mix_params)
        xs, *ffn_w = _ffn(xs, mod_dec, g_pre_ffn, g_post_ffn, (w_gu, w_down), l,
                          tm=md, tf=512, per_row=True, tiles_per_seq=1)
        cs_l.append(ext[:, t_new:])
        q, k, v, u = _inproj(xp, mod, g_pre_mix, w_in_b, l, tm=tm_i, per_row=False, tiles_per_seq=seq // tm_i)
        o, wkp, wvp = _attn_prompt(slopes, q.reshape(bp, seq, ATT_W), k.reshape(bp, seq, ATT_W),
                                   v.reshape(bp, seq, ATT_W), wkp, wvp, l, depth)
        xp = _mixout_prompt(xp, o.reshape(bp * seq, ATT_W), u, mod, l=l, tm=tm_p, tiles_per_seq=tps, **mix_params)
        xp = _ffn(xp, mod, g_pre_ffn, g_post_ffn, tuple(ffn_w), l,
                  tm=tm_f, tf=512, per_row=False, tiles_per_seq=seq // tm_f)
        cp_l.append(u.reshape(bp, seq, CONV_CH)[:, seq - (CONV_K - 1):])
    return (xp.reshape(bp, seq, D_MODEL), xs.reshape(bd, t_new, D_MODEL),
            _position_major(wkp), _position_major(wvp), jnp.stack(cp_l),
            _position_major(wks), _position_major(wvs), jnp.stack(cs_l))
```

```python
import functools
import math

import numpy as np
import jax
import jax.numpy as jnp
from jax import lax
from jax.experimental import pallas as pl
from jax.experimental.pallas import tpu as pltpu

D_MODEL = 2048
ATT_W = 1024
HEAD_DIM = 64
N_HEADS = 16
CONV_CH = 1024
CONV_K = 31
D_FF = 5632
N_MOD = 6
EPS = 1e-6
ATTN_SCALE = HEAD_DIM ** -0.5
LOG2E = math.log2(math.e)
QK_SCALE = ATTN_SCALE * LOG2E
NEG = -1e30
BRANCHES = ((128, 1), (512, 4), (2048, 16))
STEPS = 128
LANES = 128
SUBLANES = 8
HALO = 32
MOD_ROWS = 16
NORM_ROWS = 16
VMEM_LIMIT = 56 * 1024 * 1024

F32 = jnp.float32
BF16 = jnp.bfloat16
NT = (((1,), (1,)), ((), ()))


def _alibi_slopes(n):
    def pow2(m):
        start = 2.0 ** (-8.0 / m)
        return [start ** (i + 1) for i in range(m)]
    if math.log2(n).is_integer():
        s = pow2(n)
    else:
        m = 2 ** math.floor(math.log2(n))
        s = pow2(m) + pow2(2 * m)[0::2][: n - m]
    return np.array(s, dtype=np.float32)


def _rms(x, g):
    return x * lax.rsqrt(jnp.mean(x * x, axis=-1, keepdims=True) + EPS) * g


def _silu(x):
    return x * jax.nn.sigmoid(x)


def _params(sem):
    return pltpu.CompilerParams(dimension_semantics=sem, vmem_limit_bytes=VMEM_LIMIT)


def _layer_resident(shape, l):
    nd = len(shape) - 1
    return pl.BlockSpec((None,) + tuple(shape[1:]), lambda *_: (l,) + (0,) * nd, pipeline_mode=pl.Buffered(1))


def _layer_vec(w, l):
    return pl.BlockSpec((None, 1, w), lambda *_: (l, 0, 0))


def _mod_kernel(c_ref, w_ref, b_ref, o_ref):
    s = _silu(c_ref[...]).astype(BF16)
    o_ref[...] = jnp.dot(s, w_ref[...].astype(BF16), preferred_element_type=F32) + b_ref[...]


def _modulation(c_all, w_mod, b_mod):
    depth = w_mod.shape[0]
    tn = 1024
    nj = D_MODEL // tn
    return pl.pallas_call(
        _mod_kernel,
        grid=(depth, N_MOD, nj),
        in_specs=[
            pl.BlockSpec((MOD_ROWS, D_MODEL), lambda l, m, j: (0, 0)),
            pl.BlockSpec((None, D_MODEL, tn), lambda l, m, j: (l, 0, m * nj + j)),
            pl.BlockSpec((None, 1, tn), lambda l, m, j: (l, 0, m * nj + j)),
        ],
        out_specs=pl.BlockSpec((None, None, MOD_ROWS, tn), lambda l, m, j: (l, m, 0, j)),
        out_shape=jax.ShapeDtypeStruct((depth, N_MOD, MOD_ROWS, D_MODEL), F32),
        compiler_params=_params(("arbitrary", "arbitrary", "arbitrary")),
        name="modulation",
    )(c_all, w_mod, b_mod.reshape(depth, 1, N_MOD * D_MODEL))


def _mod_rows(ref, per_row, tiles_per_seq):
    if per_row:
        return ref[...]
    return ref[pl.ds(pl.program_id(0) // tiles_per_seq, 1), :]


def _mod_spec(per_row, tm, l, m):
    if per_row:
        return pl.BlockSpec((None, None, tm, D_MODEL), lambda i, *_: (l, m, i, 0))
    return pl.BlockSpec((None, None, MOD_ROWS, D_MODEL), lambda i, *_: (l, m, 0, 0))


def _inproj_kernel(x_ref, sc_ref, sh_ref, g_ref, w_ref, q_ref, k_ref, v_ref, u_ref, *, per_row, tiles_per_seq):
    sc = _mod_rows(sc_ref, per_row, tiles_per_seq)
    sh = _mod_rows(sh_ref, per_row, tiles_per_seq)
    h = (_rms(x_ref[...], g_ref[...]) * (1.0 + sc) + sh).astype(BF16)

    def proj(c):
        return jnp.dot(h, w_ref[:, c * ATT_W:(c + 1) * ATT_W], preferred_element_type=F32)

    q_ref[...] = proj(0)
    k_ref[...] = proj(1)
    v_ref[...] = proj(2)
    u_ref[...] = proj(3) * jax.nn.sigmoid(proj(4))


def _inproj(x, mod, g_pre, w_in, l, *, tm, per_row, tiles_per_seq):
    m_rows = x.shape[0]
    row = lambda w: pl.BlockSpec((tm, w), lambda i: (i, 0))
    out = jax.ShapeDtypeStruct((m_rows, ATT_W), F32)
    return pl.pallas_call(
        functools.partial(_inproj_kernel, per_row=per_row, tiles_per_seq=tiles_per_seq),
        grid=(m_rows // tm,),
        in_specs=[
            row(D_MODEL),
            _mod_spec(per_row, tm, l, 1),
            _mod_spec(per_row, tm, l, 0),
            _layer_vec(D_MODEL, l),
            _layer_resident(w_in.shape, l),
        ],
        out_specs=[row(ATT_W)] * 4,
        out_shape=[out] * 4,
        compiler_params=_params(("arbitrary",)),
        name="inproj",
    )(x, mod, mod, g_pre, w_in)


def _attn_prompt_kernel(slopes_ref, q_ref, k_ref, v_ref, *rest, seq, chained):
    o_ref, kt_ref, vt_ref, x4_scr, res_scr, bias_scr = rest[2:] if chained else rest
    hp = pl.program_id(1)
    lane = lax.broadcasted_iota(jnp.int32, (1, LANES), 1)
    head0 = lane < HEAD_DIM
    plane = seq // 4

    qi = lax.broadcasted_iota(jnp.int32, (STEPS, 2 * STEPS), 0)
    ki = lax.broadcasted_iota(jnp.int32, (STEPS, 2 * STEPS), 1)
    j = STEPS + qi - ki
    valid = (j >= 0) & (j <= STEPS)
    jf = j.astype(F32)
    for h in range(2):
        slope = slopes_ref[2 * hp + h]
        for bi, (_, d) in enumerate(BRANCHES):
            bias_scr[bi, h * STEPS:(h + 1) * STEPS, :] = jnp.where(valid, -(slope * d * LOG2E) * jf, NEG)

    for c, ref in enumerate((q_ref, k_ref, v_ref)):
        for r in range(4):
            x = ref[pl.ds(r, plane, stride=4), :]
            x4_scr[c, r * plane:(r + 1) * plane, :] = x * QK_SCALE if c == 0 else x
    for c in range(seq // LANES):
        cols = slice(c * LANES, (c + 1) * LANES)
        kt_ref[:, cols] = k_ref[cols, :].T
        vt_ref[:, cols] = v_ref[cols, :].T

    def attend(q, k_parts, v_parts, bi):
        nk = STEPS * len(k_parts)
        kc = (jnp.concatenate(k_parts, axis=0) if len(k_parts) > 1 else k_parts[0]).astype(BF16)
        vc = (jnp.concatenate(v_parts, axis=0) if len(v_parts) > 1 else v_parts[0]).astype(BF16)
        vc1 = jnp.concatenate([vc, jnp.ones((nk, LANES), BF16)], axis=1)
        q2 = jnp.concatenate([jnp.where(head0, q, 0.0), jnp.where(head0, 0.0, q)], axis=0).astype(BF16)
        s = lax.dot_general(q2, kc, NT, preferred_element_type=F32)
        s = s + (bias_scr[bi] if nk == 2 * STEPS else bias_scr[bi, :, STEPS:])
        m = jnp.max(s, axis=-1, keepdims=True)
        p = jnp.exp2(s - m).astype(BF16)
        r = jnp.dot(p, vc1, preferred_element_type=F32)
        num = jnp.where(head0, r[:STEPS, :LANES], r[STEPS:, :LANES])
        den = jnp.where(head0, r[:STEPS, LANES:], r[STEPS:, LANES:])
        mx = jnp.where(head0, m[:STEPS], m[STEPS:])
        return num * (1.0 / den), mx + jnp.log2(den)

    def block1(start, first):
        own = pl.ds(start, STEPS)
        q = q_ref[own, :] * QK_SCALE
        if first:
            o, lse = attend(q, [k_ref[own, :]], [v_ref[own, :]], 0)
        else:
            prev = pl.ds(start - STEPS, STEPS)
            o, lse = attend(q, [k_ref[prev, :], k_ref[own, :]], [v_ref[prev, :], v_ref[own, :]], 0)
        res_scr[0, own, :] = o
        res_scr[1, own, :] = lse

    for n in range(seq // STEPS):
        block1(n * STEPS, n == 0)

    def block4(base, first):
        own = pl.ds(base, STEPS)
        q = x4_scr[0, own, :]
        if first:
            o, lse = attend(q, [x4_scr[1, own, :]], [x4_scr[2, own, :]], 1)
        else:
            prev = pl.ds(base - STEPS, STEPS)
            o, lse = attend(q, [x4_scr[1, prev, :], x4_scr[1, own, :]], [x4_scr[2, prev, :], x4_scr[2, own, :]], 1)
        res_scr[2, own, :] = o
        res_scr[3, own, :] = lse

    for n in range(plane // STEPS):
        for r in range(4):
            block4(r * plane + n * STEPS, n == 0)

    for r4 in range(4):
        for rp in range(4):
            rows = pl.ds(r4 * plane + rp, STEPS, stride=4)
            o, lse = attend(x4_scr[0, rows, :], [x4_scr[1, rows, :]], [x4_scr[2, rows, :]], 2)
            res_scr[4, rows, :] = o
            res_scr[5, rows, :] = lse

    def merge(r4, carry):
        for c in range(plane // STEPS):
            mine = pl.ds(pl.multiple_of(r4 * plane + c * STEPS, STEPS), STEPS)
            nat = pl.ds(r4 + 4 * c * STEPS, STEPS, stride=4)
            l0, l1, l2 = res_scr[1, nat, :], res_scr[3, mine, :], res_scr[5, mine, :]
            m = jnp.maximum(jnp.maximum(l0, l1), l2)
            e0, e1, e2 = jnp.exp2(l0 - m), jnp.exp2(l1 - m), jnp.exp2(l2 - m)
            num = e0 * res_scr[0, nat, :] + e1 * res_scr[2, mine, :] + e2 * res_scr[4, mine, :]
            o_ref[nat, :] = num * (1.0 / (e0 + e1 + e2))
        return carry
    lax.fori_loop(0, 4, merge, 0)


def _attn_prompt(slopes, q, k, v, win_k, win_v, l, depth):
    b, seq, _ = q.shape
    chained = win_k is not None
    blk = pl.BlockSpec((None, seq, LANES), lambda i, j: (i, 0, j))
    win = pl.BlockSpec((None, None, LANES, seq), lambda i, j: (l, i, j, 0))
    win_shape = jax.ShapeDtypeStruct((depth, b, ATT_W, seq), F32)
    any_spec = pl.BlockSpec(memory_space=pl.ANY)
    return pl.pallas_call(
        functools.partial(_attn_prompt_kernel, seq=seq, chained=chained),
        grid=(b, ATT_W // LANES),
        in_specs=[pl.BlockSpec(memory_space=pltpu.SMEM), blk, blk, blk] + ([any_spec, any_spec] if chained else []),
        out_specs=[blk, win, win],
        out_shape=[jax.ShapeDtypeStruct(q.shape, F32), win_shape, win_shape],
        input_output_aliases={4: 1, 5: 2} if chained else {},
        scratch_shapes=[
            pltpu.VMEM((3, seq, LANES), F32),
            pltpu.VMEM((6, seq, LANES), F32),
            pltpu.VMEM((3, 2 * STEPS, 2 * STEPS), F32),
        ],
        compiler_params=_params(("arbitrary", "arbitrary")),
        name="attn_prompt",
    )(slopes, q, k, v, *((win_k, win_v) if chained else ()))


def _branch_count(delta):
    cnt = jnp.zeros(delta.shape, F32)
    for w, d in BRANCHES:
        ok = (delta >= 0) & (delta <= w) & ((delta & (d - 1)) == 0)
        cnt = cnt + ok.astype(F32)
    return cnt


def _attn_decode_kernel(slopes_ref, q_ref, kn_ref, vn_ref, kc_ref, vc_ref, *rest, n_past, t_new, chained, pairs):
    o_ref, ko_ref, vo_ref = rest[2:] if chained else rest
    lane = lax.broadcasted_iota(jnp.int32, (1, LANES), 1)
    head0 = lane < HEAD_DIM
    keep = n_past - t_new

    rows = 2 * t_new
    ti = lax.broadcasted_iota(jnp.int32, (rows, n_past), 0) % t_new
    ci = lax.broadcasted_iota(jnp.int32, (rows, n_past), 1)
    delta_c = n_past + ti - ci
    cnt_c = _branch_count(delta_c)
    delta_w = ti - (ci - keep)
    cnt_w = jnp.where(ci >= keep, _branch_count(delta_w), 0.0)
    row = lax.broadcasted_iota(jnp.int32, (rows, 1), 0)

    for sub in range(pairs):
        hp = pl.program_id(1) * pairs + sub
        feat = slice(sub * LANES, (sub + 1) * LANES)

        def shifted(cache_ref, new_ref, out_ref):
            rolled = pltpu.roll(cache_ref[feat, :], n_past - t_new, axis=1)
            new_t = jnp.concatenate([new_ref[:, feat], jnp.zeros((LANES - t_new, LANES), F32)], axis=0).T
            new_t = pltpu.roll(new_t, LANES - t_new, axis=1)
            out_ref[feat, :n_past - LANES] = rolled[:, :n_past - LANES]
            out_ref[feat, n_past - LANES:] = jnp.where(lane >= LANES - t_new, new_t, rolled[:, n_past - LANES:])
        shifted(kc_ref, kn_ref, ko_ref)
        shifted(vc_ref, vn_ref, vo_ref)

        slope = jnp.where(row < t_new, slopes_ref[2 * hp], slopes_ref[2 * hp + 1])
        q = q_ref[:, feat] * ATTN_SCALE
        q2 = jnp.concatenate([jnp.where(head0, q, 0.0), jnp.where(head0, 0.0, q)], axis=0).astype(BF16)
        s_c = jnp.dot(q2, kc_ref[feat, :].astype(BF16), preferred_element_type=F32)
        s_w = jnp.dot(q2, ko_ref[feat, :].astype(BF16), preferred_element_type=F32)
        s_c = jnp.where(cnt_c > 0, s_c - slope * delta_c.astype(F32), NEG)
        s_w = jnp.where(cnt_w > 0, s_w - slope * delta_w.astype(F32), NEG)
        m = jnp.maximum(jnp.max(s_c, axis=-1, keepdims=True), jnp.max(s_w, axis=-1, keepdims=True))
        p_c = cnt_c * jnp.exp(s_c - m)
        p_w = cnt_w * jnp.exp(s_w - m)
        den = jnp.sum(p_c, axis=-1, keepdims=True) + jnp.sum(p_w, axis=-1, keepdims=True)
        num = (lax.dot_general(p_c.astype(BF16), vc_ref[feat, :].astype(BF16), NT, preferred_element_type=F32)
               + lax.dot_general(p_w.astype(BF16), vo_ref[feat, :].astype(BF16), NT, preferred_element_type=F32))
        o = num * (1.0 / den)
        o_ref[:, feat] = jnp.where(head0, o[:t_new], o[t_new:])


def _attn_decode(slopes, q, k, v, cache_k, cache_v, win_k, win_v, l):
    b, t_new, _ = q.shape
    depth, _, _, n_past = cache_k.shape
    chained = win_k is not None
    pairs = 2
    width = pairs * LANES
    new = pl.BlockSpec((None, t_new, width), lambda i, j: (i, 0, j))
    win = pl.BlockSpec((None, None, width, n_past), lambda i, j: (l, i, j, 0))
    win_shape = jax.ShapeDtypeStruct(cache_k.shape, F32)
    any_spec = pl.BlockSpec(memory_space=pl.ANY)
    return pl.pallas_call(
        functools.partial(_attn_decode_kernel, n_past=n_past, t_new=t_new, chained=chained, pairs=pairs),
        grid=(b, ATT_W // width),
        in_specs=[pl.BlockSpec(memory_space=pltpu.SMEM), new, new, new, win, win]
        + ([any_spec, any_spec] if chained else []),
        out_specs=[new, win, win],
        out_shape=[jax.ShapeDtypeStruct(q.shape, F32), win_shape, win_shape],
        input_output_aliases={6: 1, 7: 2} if chained else {},
        compiler_params=_params(("arbitrary", "arbitrary")),
        name="attn_decode",
    )(slopes, q, k, v, cache_k, cache_v, *((win_k, win_v) if chained else ()))


def _conv_act(y, cb_ref, lng_ref, lnb_ref):
    y = y + cb_ref[...]
    mu = jnp.mean(y, axis=-1, keepdims=True)
    yc = y - mu
    yn = yc * lax.rsqrt(jnp.mean(yc * yc, axis=-1, keepdims=True) + EPS) * lng_ref[...] + lnb_ref[...]
    return _silu(yn).astype(BF16)


def _mix_out(x, o, ya, gt, gao_ref, wout_ref, gpost_ref):
    on = _rms(o, gao_ref[...]).astype(BF16)
    mix = (jnp.dot(on, wout_ref[0:ATT_W, :], preferred_element_type=F32)
           + jnp.dot(ya, wout_ref[ATT_W:, :], preferred_element_type=F32))
    return x + gt * _rms(mix, gpost_ref[...])


def _mixout_prompt_kernel(x_ref, o_ref, u_ref, halo_ref, gt_ref, gao_ref, cw_ref, cb_ref, lng_ref, lnb_ref,
                          wout_ref, gpost_ref, out_ref, ext_scr, y_scr, *, tm, tiles_per_seq):
    first = pl.program_id(0) % tiles_per_seq == 0
    n_ct = CONV_CH // LANES
    for c in range(n_ct):
        lanes = slice(c * LANES, (c + 1) * LANES)
        ext_scr[c, 0:HALO, :] = jnp.where(first, 0.0, halo_ref[:, lanes])
        ext_scr[c, HALO:HALO + tm, :] = u_ref[:, lanes]
    off = HALO - (CONV_K - 1)
    half = tm // 2
    for c in range(n_ct):
        lanes = slice(c * LANES, (c + 1) * LANES)
        acc = [None, None]
        for k in range(CONV_K):
            w = jnp.broadcast_to(cw_ref[k:k + 1, lanes], (half, LANES))
            for par in range(2):
                term = w * ext_scr[c, pl.ds(par + off + k, half, stride=2), :]
                acc[par] = term if acc[par] is None else acc[par] + term
        for par in range(2):
            y_scr[c, pl.ds(par, half, stride=2), :] = acc[par]
    y = jnp.concatenate([y_scr[c] for c in range(n_ct)], axis=1)
    ya = _conv_act(y, cb_ref, lng_ref, lnb_ref)
    gt = _mod_rows(gt_ref, False, tiles_per_seq)
    out_ref[...] = _mix_out(x_ref[...], o_ref[...], ya, gt, gao_ref, wout_ref, gpost_ref)


def _mixout_decode_kernel(x_ref, o_ref, ext_ref, gt_ref, gao_ref, cw_ref, cb_ref, lng_ref, lnb_ref,
                          wout_ref, gpost_ref, out_ref, y_scr, *, n_seq, t_new):
    cw = 256
    for b in range(n_seq):
        for c0 in range(0, CONV_CH, cw):
            acc = None
            for k in range(CONV_K):
                term = cw_ref[k:k + 1, c0:c0 + cw] * ext_ref[b, k:k + t_new, c0:c0 + cw]
                acc = term if acc is None else acc + term
            y_scr[b * t_new:(b + 1) * t_new, c0:c0 + cw] = acc
    ya = _conv_act(y_scr[...], cb_ref, lng_ref, lnb_ref)
    out_ref[...] = _mix_out(x_ref[...], o_ref[...], ya, gt_ref[...], gao_ref, wout_ref, gpost_ref)


def _mixout_prompt(x, o, u, mod, g_attn_out, conv_w, conv_b, ln_g, ln_b, w_out, g_post, l, *, tm, tiles_per_seq):
    m_rows = x.shape[0]
    row = lambda w: pl.BlockSpec((tm, w), lambda i: (i, 0))
    halo = pl.BlockSpec((HALO, CONV_CH), lambda i: (jnp.maximum(i * (tm // HALO) - 1, 0), 0))
    return pl.pallas_call(
        functools.partial(_mixout_prompt_kernel, tm=tm, tiles_per_seq=tiles_per_seq),
        grid=(m_rows // tm,),
        in_specs=[row(D_MODEL), row(ATT_W), row(CONV_CH), halo, _mod_spec(False, tm, l, 2), _layer_vec(ATT_W, l),
                  pl.BlockSpec((None, CONV_K, CONV_CH), lambda i: (l, 0, 0)),
                  _layer_vec(CONV_CH, l), _layer_vec(CONV_CH, l), _layer_vec(CONV_CH, l),
                  _layer_resident(w_out.shape, l), _layer_vec(D_MODEL, l)],
        out_specs=row(D_MODEL),
        out_shape=jax.ShapeDtypeStruct(x.shape, F32),
        scratch_shapes=[pltpu.VMEM((CONV_CH // LANES, HALO + tm, LANES), F32),
                        pltpu.VMEM((CONV_CH // LANES, tm, LANES), F32)],
        compiler_params=_params(("arbitrary",)),
        name="mixout_prompt",
    )(x, o, u, u, mod, g_attn_out, conv_w, conv_b, ln_g, ln_b, w_out, g_post)


def _mixout_decode(x, o, ext, mod_rows, g_attn_out, conv_w, conv_b, ln_g, ln_b, w_out, g_post, l):
    m_rows = x.shape[0]
    n_seq, ext_len, _ = ext.shape
    t_new = ext_len - (CONV_K - 1)
    row = lambda w: pl.BlockSpec((m_rows, w), lambda i: (0, 0))
    return pl.pallas_call(
        functools.partial(_mixout_decode_kernel, n_seq=n_seq, t_new=t_new),
        grid=(1,),
        in_specs=[row(D_MODEL), row(ATT_W), pl.BlockSpec(ext.shape, lambda i: (0, 0, 0)),
                  _mod_spec(True, m_rows, l, 2), _layer_vec(ATT_W, l),
                  pl.BlockSpec((None, CONV_K, CONV_CH), lambda i: (l, 0, 0)),
                  _layer_vec(CONV_CH, l), _layer_vec(CONV_CH, l), _layer_vec(CONV_CH, l),
                  _layer_resident(w_out.shape, l), _layer_vec(D_MODEL, l)],
        out_specs=row(D_MODEL),
        out_shape=jax.ShapeDtypeStruct(x.shape, F32),
        scratch_shapes=[pltpu.VMEM((m_rows, CONV_CH), F32)],
        compiler_params=_params(("arbitrary",)),
        name="mixout_decode",
    )(x, o, ext, mod_rows, g_attn_out, conv_w, conv_b, ln_g, ln_b, w_out, g_post)


def _ffn_kernel(x_ref, sc_ref, sh_ref, gt_ref, gpre_ref, gpost_ref, wg_ref, wu_ref, wd_ref, out_ref, *rest,
                per_row, tiles_per_seq, emit_w):
    h_scr, acc_scr = rest[-2:]
    f = pl.program_id(1)
    tm = x_ref.shape[0]

    def by_chunks(fn):
        for c in range(tm // NORM_ROWS):
            rows = slice(c * NORM_ROWS, (c + 1) * NORM_ROWS)
            fn(rows, lambda ref: ref[rows, :] if per_row else _mod_rows(ref, False, tiles_per_seq))

    @pl.when(f == 0)
    def _():
        def chunk(rows, mod):
            x = x_ref[rows, :]
            inv = lax.rsqrt(jnp.mean(x * x, axis=-1, keepdims=True) + EPS)
            h_scr[rows, :] = (x * inv * (gpre_ref[...] * (1.0 + mod(sc_ref))) + mod(sh_ref)).astype(BF16)
            acc_scr[rows, :] = jnp.zeros((NORM_ROWS, D_MODEL), F32)
        by_chunks(chunk)

    wg, wu, wd = wg_ref[...], wu_ref[...], wd_ref[...]
    if emit_w:
        wg, wu, wd = wg.astype(BF16), wu.astype(BF16), wd.astype(BF16)
        rest[0][...] = wg
        rest[1][...] = wu
        rest[2][...] = wd
    h = h_scr[...]
    gate = jnp.dot(h, wg, preferred_element_type=F32)
    up = jnp.dot(h, wu, preferred_element_type=F32)
    act = (_silu(gate) * up).astype(BF16)
    acc_scr[...] += jnp.dot(act, wd, preferred_element_type=F32)

    @pl.when(f == pl.num_programs(1) - 1)
    def _():
        def chunk(rows, mod):
            acc = acc_scr[rows, :]
            inv = lax.rsqrt(jnp.mean(acc * acc, axis=-1, keepdims=True) + EPS)
            out_ref[rows, :] = x_ref[rows, :] + acc * inv * (gpost_ref[...] * mod(gt_ref))
        by_chunks(chunk)


def _ffn(x, mod, g_pre, g_post, weights, l, *, tm, tf, per_row, tiles_per_seq):
    m_rows = x.shape[0]
    nf = D_FF // tf
    emit_w = len(weights) == 2
    row = pl.BlockSpec((tm, D_MODEL), lambda i, f: (i, 0))
    col = pl.BlockSpec((D_MODEL, tf), lambda i, f: (0, f))
    chunk = pl.BlockSpec((tf, D_MODEL), lambda i, f: (f, 0))
    if emit_w:
        assert m_rows == tm
        w_gu, w_down = weights
        w_args = (w_gu, w_gu, w_down)
        w_specs = [pl.BlockSpec((None, D_MODEL, tf), lambda i, f: (l, 0, f)),
                   pl.BlockSpec((None, D_MODEL, tf), lambda i, f: (l, 0, f + nf)),
                   pl.BlockSpec((None, tf, D_MODEL), lambda i, f: (l, f, 0))]
        out_specs = [row, col, col, chunk]
        out_shape = [jax.ShapeDtypeStruct(x.shape, F32), jax.ShapeDtypeStruct((D_MODEL, D_FF), BF16),
                     jax.ShapeDtypeStruct((D_MODEL, D_FF), BF16), jax.ShapeDtypeStruct((D_FF, D_MODEL), BF16)]
    else:
        w_args = weights
        w_specs = [col, col, chunk]
        out_specs = row
        out_shape = jax.ShapeDtypeStruct(x.shape, F32)
    return pl.pallas_call(
        functools.partial(_ffn_kernel, per_row=per_row, tiles_per_seq=tiles_per_seq, emit_w=emit_w),
        grid=(m_rows // tm, nf),
        in_specs=[row, _mod_spec(per_row, tm, l, 4), _mod_spec(per_row, tm, l, 3), _mod_spec(per_row, tm, l, 5),
                  _layer_vec(D_MODEL, l), _layer_vec(D_MODEL, l)] + w_specs,
        out_specs=out_specs,
        out_shape=out_shape,
        scratch_shapes=[pltpu.VMEM((tm, D_MODEL), BF16), pltpu.VMEM((tm, D_MODEL), F32)],
        compiler_params=_params(("arbitrary", "arbitrary")),
        name="ffn",
    )(x, mod, mod, mod, g_pre, g_post, *w_args)


def _feature_major(win):
    depth, b, p, h, dh = win.shape
    return jnp.transpose(win, (0, 1, 3, 4, 2)).reshape(depth, b, h * dh, p)


def _position_major(win):
    depth, b, w, p = win.shape
    return jnp.transpose(win.reshape(depth, b, N_HEADS, HEAD_DIM, p), (0, 1, 4, 2, 3))


def kernel(x_prompt, x_sample, c_prompt, c_sample, cache_win_k, cache_win_v, state_conv, g_pre_mix, g_post_mix, g_pre_ffn, g_post_ffn, w_mod, b_mod, w_in, g_attn_out, conv_w, conv_b, conv_ln_g, conv_ln_b, w_out, w_gu, w_down):
    depth = w_mod.shape[0]
    bp, seq, _ = x_prompt.shape
    bd, t_new, _ = x_sample.shape
    n_past = cache_win_k.shape[2]
    assert bp + bd <= MOD_ROWS and t_new == SUBLANES
    assert seq == STEPS * BRANCHES[-1][1] and n_past >= BRANCHES[-1][0] and seq >= BRANCHES[-1][0]

    slopes = jnp.asarray(_alibi_slopes(N_HEADS))
    c_all = jnp.zeros((MOD_ROWS, D_MODEL), F32).at[:bp].set(c_prompt).at[bp:bp + bd].set(c_sample)
    mod = _modulation(c_all, w_mod, b_mod)
    mod_dec = jnp.repeat(mod[:, :, bp:bp + bd], t_new, axis=2)

    w_in_b, w_out_b = w_in.astype(BF16), w_out.astype(BF16)
    vec = lambda p: p.reshape(depth, 1, p.shape[-1])
    g_pre_mix, g_post_mix, g_pre_ffn, g_post_ffn = vec(g_pre_mix), vec(g_post_mix), vec(g_pre_ffn), vec(g_post_ffn)
    mix_params = dict(g_attn_out=vec(g_attn_out), conv_w=conv_w, conv_b=vec(conv_b), ln_g=vec(conv_ln_g),
                      ln_b=vec(conv_ln_b), w_out=w_out_b, g_post=g_post_mix)
    cache_k = _feature_major(cache_win_k)
    cache_v = _feature_major(cache_win_v)

    tm_i = 512
    tm_p = 256
    tps = seq // tm_p
    tm_f = 512
    xp = x_prompt.reshape(bp * seq, D_MODEL)
    xs = x_sample.reshape(bd * t_new, D_MODEL)
    md = bd * t_new
    wkp = wvp = wks = wvs = None
    cp_l, cs_l = [], []
    for l in range(depth):
        q, k, v, u = _inproj(xs, mod_dec, g_pre_mix, w_in_b, l, tm=md, per_row=True, tiles_per_seq=1)
        o, wks, wvs = _attn_decode(slopes, q.reshape(bd, t_new, ATT_W), k.reshape(bd, t_new, ATT_W),
                                   v.reshape(bd, t_new, ATT_W), cache_k, cache_v, wks, wvs, l)
        ext = jnp.concatenate([state_conv[l], u.reshape(bd, t_new, CONV_CH)], axis=1)
        xs = _mixout_decode(xs, o.reshape(md, ATT_W), ext, mod_dec, l=l, **mix_params)
        xs, *ffn_w = _ffn(xs, mod_dec, g_pre_ffn, g_post_ffn, (w_gu, w_down), l,
                          tm=md, tf=512, per_row=True, tiles_per_seq=1)
        cs_l.append(ext[:, t_new:])
        q, k, v, u = _inproj(xp, mod, g_pre_mix, w_in_b, l, tm=tm_i, per_row=False, tiles_per_seq=seq // tm_i)
        o, wkp, wvp = _attn_prompt(slopes, q.reshape(bp, seq, ATT_W), k.reshape(bp, seq, ATT_W),
                                   v.reshape(bp, seq, ATT_W), wkp, wvp, l, depth)
        xp = _mixout_prompt(xp, o.reshape(bp * seq, ATT_W), u, mod, l=l, tm=tm_p, tiles_per_seq=tps, **mix_params)
        xp = _ffn(xp, mod, g_pre_ffn, g_post_ffn, tuple(ffn_w), l,
                  tm=tm_f, tf=512, per_row=False, tiles_per_seq=seq // tm_f)
        cp_l.append(u.reshape(bp, seq, CONV_CH)[:, seq - (CONV_K - 1):])
    return (xp.reshape(bp, seq, D_MODEL), xs.reshape(bd, t_new, D_MODEL),
            _position_major(wkp), _position_major(wvp), jnp.stack(cp_l),
            _position_major(wks), _position_major(wvs), jnp.stack(cs_l))
```
